```python
import jax, jax.numpy as jnp
from jax import lax
import numpy as np

D_MODEL = 2048
BATCH = 2
SEQ = 8192
DEPTH = 4

CHUNK = 64
Q_BLOCK = 128
SB_HEADS = 8
SB_HEAD_DIM = 128
SB_WIDTH = SB_HEADS * SB_HEAD_DIM
SGU_GROUPS = 8
SGU_GROUP_DIM = 128
SGU_WIDTH = SGU_GROUPS * SGU_GROUP_DIM
SGU_LEN = 128
D_FF = 4 * D_MODEL
IN_COLS = 3 * SB_WIDTH + 2 * SGU_WIDTH + 2 * D_MODEL
EPS = 1e-6

kernel_name = "hybrid_stickbreak_sgu_block"


def rms_norm(x, g):
    xf = x.astype(jnp.float32)
    y = xf * lax.rsqrt(jnp.mean(xf * xf, axis=-1, keepdims=True) + EPS)
    return (y * g.astype(jnp.float32)).astype(x.dtype)


def layer_norm(x, g, b):
    xf = x.astype(jnp.float32)
    mu = jnp.mean(xf, axis=-1, keepdims=True)
    xc = xf - mu
    y = xc * lax.rsqrt(jnp.mean(xc * xc, axis=-1, keepdims=True) + EPS)
    return (y * g.astype(jnp.float32) + b.astype(jnp.float32)).astype(x.dtype)


def stick_breaking_attention(q, k, v):
    seq = q.shape[2]
    scale = SB_HEAD_DIM ** -0.5
    outs = []
    for blk in range(seq // Q_BLOCK):
        q0 = blk * Q_BLOCK
        kend = q0 + Q_BLOCK
        qb = q[:, :, q0:kend].astype(jnp.float32)
        kb = k[:, :, :kend].astype(jnp.float32)
        vb = v[:, :, :kend]
        z = jnp.einsum('bhtd,bhsd->bhts', qb, kb) * scale
        t_idx = q0 + jnp.arange(Q_BLOCK)[:, None]
        s_idx = jnp.arange(kend)[None, :]
        past = s_idx < t_idx
        log_keep = jnp.where(past, jax.nn.log_sigmoid(-z), 0.0)
        tail = lax.cumsum(log_keep, axis=3, reverse=True) - log_keep
        log_a = jax.nn.log_sigmoid(z) + tail
        a = jnp.where(past, jnp.exp(log_a), 0.0)
        outs.append(jnp.einsum('bhts,bhsd->bhtd', a.astype(v.dtype), vb))
    return jnp.concatenate(outs, axis=2)


def spatial_gating(u, v, ln_g, ln_b, w_s, b_s):
    bsz, seq, _ = v.shape
    v = layer_norm(v, ln_g, ln_b)
    vc = v.reshape(bsz, seq // SGU_LEN, SGU_LEN, SGU_GROUPS, SGU_GROUP_DIM)
    pos = jnp.arange(SGU_LEN)
    mask = (pos[None, :] // CHUNK) <= (pos[:, None] // CHUNK)
    w = jnp.where(mask[None], w_s, jnp.zeros_like(w_s))
    mixed = jnp.einsum('gij,bcjgd->bcigd', w, vc) + b_s.T[:, :, None]
    return u * mixed.reshape(bsz, seq, SGU_WIDTH)


def setup_inputs(seed: int = 0) -> dict:
    key = jax.random.key(seed)
    ks = jax.random.split(key, 16)
    f32 = jnp.float32
    nrm = lambda k, shape, s: jax.random.normal(k, shape, f32) * s
    return {
        "x": nrm(ks[0], (BATCH, SEQ, D_MODEL), 1.0),
        "g_mix": 1.0 + nrm(ks[1], (DEPTH, D_MODEL), 0.05),
        "w_in": nrm(ks[2], (DEPTH, D_MODEL, IN_COLS), D_MODEL ** -0.5),
        "g_q": 1.0 + nrm(ks[3], (DEPTH, SB_HEADS, SB_HEAD_DIM), 0.05),
        "g_k": 1.0 + nrm(ks[4], (DEPTH, SB_HEADS, SB_HEAD_DIM), 0.05),
        "sgu_ln_g": 1.0 + nrm(ks[5], (DEPTH, SGU_WIDTH), 0.05),
        "sgu_ln_b": nrm(ks[6], (DEPTH, SGU_WIDTH), 0.02),
        "w_spatial": nrm(ks[7], (DEPTH, SGU_GROUPS, SGU_LEN, SGU_LEN), SGU_LEN ** -0.5),
        "b_spatial": 1.0 + nrm(ks[8], (DEPTH, SGU_GROUPS, SGU_LEN), 0.05),
        "w_oa": nrm(ks[9], (DEPTH, SB_WIDTH, D_MODEL), SB_WIDTH ** -0.5),
        "w_ob": nrm(ks[10], (DEPTH, SGU_WIDTH, D_MODEL), SGU_WIDTH ** -0.5),
        "w_out": nrm(ks[11], (DEPTH, D_MODEL, D_MODEL), D_MODEL ** -0.5),
        "g_ff": 1.0 + nrm(ks[12], (DEPTH, D_MODEL), 0.05),
        "w_ff1": nrm(ks[13], (DEPTH, D_MODEL, D_FF), D_MODEL ** -0.5),
        "w_ff2": nrm(ks[14], (DEPTH, D_FF, D_MODEL), D_FF ** -0.5),
    }


def reference(x, g_mix, w_in, g_q, g_k, sgu_ln_g, sgu_ln_b, w_spatial, b_spatial,
              w_oa, w_ob, w_out, g_ff, w_ff1, w_ff2):
    bsz, seq, _ = x.shape
    splits = [SB_WIDTH, 2 * SB_WIDTH, 3 * SB_WIDTH,
              3 * SB_WIDTH + SGU_WIDTH, 3 * SB_WIDTH + 2 * SGU_WIDTH,
              3 * SB_WIDTH + 2 * SGU_WIDTH + D_MODEL]
    for l in range(DEPTH):
        h = rms_norm(x, g_mix[l])
        proj = h @ w_in[l]
        q, k, v_sb, u, v_sg, gate_a, gate_b = jnp.split(proj, splits, axis=-1)

        q = rms_norm(q.reshape(bsz, seq, SB_HEADS, SB_HEAD_DIM), g_q[l])
        k = rms_norm(k.reshape(bsz, seq, SB_HEADS, SB_HEAD_DIM), g_k[l])
        v_sb = v_sb.reshape(bsz, seq, SB_HEADS, SB_HEAD_DIM)
        o = stick_breaking_attention(q.transpose(0, 2, 1, 3), k.transpose(0, 2, 1, 3),
                                     v_sb.transpose(0, 2, 1, 3))
        y_a = o.transpose(0, 2, 1, 3).reshape(bsz, seq, SB_WIDTH) @ w_oa[l]

        u = jax.nn.gelu(u, approximate=False)
        v_sg = jax.nn.gelu(v_sg, approximate=False)
        s = spatial_gating(u, v_sg, sgu_ln_g[l], sgu_ln_b[l], w_spatial[l], b_spatial[l])
        y_b = s @ w_ob[l]

        merged = jax.nn.sigmoid(gate_a) * y_a + jax.nn.sigmoid(gate_b) * y_b
        x = x + merged @ w_out[l]

        h2 = rms_norm(x, g_ff[l])
        x = x + jnp.square(jax.nn.relu(h2 @ w_ff1[l])) @ w_ff2[l]
    return x
```

```python
import functools

import jax
import jax.numpy as jnp
from jax import lax
from jax.experimental import pallas as pl
from jax.experimental.pallas import tpu as pltpu

EPS = 1e-6
HEAD_DIM = 128
SGU_LEN = 128
SGU_CHUNK = 64
SEG = 1024

VMEM_LIMIT_BYTES = 56 * 1024 * 1024

BF16 = jnp.bfloat16
F32 = jnp.float32


def _params(sem):
    return pltpu.CompilerParams(dimension_semantics=sem, vmem_limit_bytes=VMEM_LIMIT_BYTES)


def _dot(a, b):
    return jnp.dot(a, b, preferred_element_type=F32)


def _gelu(x):
    return 0.5 * x * (1.0 + lax.erf(x * (2.0 ** -0.5)))


def _in_proj_kernel(x_ref, g_ref, w_ref, gq_ref, gk_ref, lng_ref, lnb_ref, o_ref, h_ref, *, scale):
    j = pl.program_id(1)

    @pl.when(j == 0)
    def _():
        x = x_ref[...]
        ms = jnp.mean(x * x, axis=-1, keepdims=True)
        h_ref[...] = (x * lax.rsqrt(ms + EPS) * g_ref[...]).astype(BF16)

    acc = _dot(h_ref[...], w_ref[...])

    def head_norm(gain_ref, mul):
        for h in range(SEG // HEAD_DIM):
            sl = slice(h * HEAD_DIM, (h + 1) * HEAD_DIM)
            blk = acc[:, sl]
            ms = jnp.mean(blk * blk, axis=-1, keepdims=True)
            y = blk * lax.rsqrt(ms + EPS) * gain_ref[:, sl]
            if mul != 1.0:
                y = y * mul
            o_ref[:, sl] = y.astype(BF16)

    @pl.when(j == 0)
    def _():
        head_norm(gq_ref, scale)

    @pl.when(j == 1)
    def _():
        head_norm(gk_ref, 1.0)

    @pl.when(j == 2)
    def _():
        o_ref[...] = acc.astype(BF16)

    @pl.when(j == 3)
    def _():
        o_ref[...] = _gelu(acc).astype(BF16)

    @pl.when(j == 4)
    def _():
        v = _gelu(acc)
        mu = jnp.mean(v, axis=-1, keepdims=True)
        vc = v - mu
        var = jnp.mean(vc * vc, axis=-1, keepdims=True)
        o_ref[...] = (vc * lax.rsqrt(var + EPS) * lng_ref[...] + lnb_ref[...]).astype(BF16)

    @pl.when(j >= 5)
    def _():
        o_ref[...] = jax.nn.sigmoid(acc).astype(BF16)


def _in_proj(x, g, w, gq, gk, lng, lnb, *, tm):
    m, d = x.shape
    n = w.shape[1]
    assert n % SEG == 0 and m % tm == 0
    row = lambda i, j: (0, 0)
    return pl.pallas_call(
        functools.partial(_in_proj_kernel, scale=HEAD_DIM ** -0.5),
        grid=(m // tm, n // SEG),
        in_specs=[
            pl.BlockSpec((tm, d), lambda i, j: (i, 0)),
            pl.BlockSpec((1, d), row),
            pl.BlockSpec((d, SEG), lambda i, j: (0, j)),
            pl.BlockSpec((1, SEG), row),
            pl.BlockSpec((1, SEG), row),
            pl.BlockSpec((1, SEG), row),
            pl.BlockSpec((1, SEG), row),
        ],
        out_specs=pl.BlockSpec((tm, SEG), lambda i, j: (i, j)),
        out_shape=jax.ShapeDtypeStruct((m, n), BF16),
        scratch_shapes=[pltpu.VMEM((tm, d), BF16)],
        compiler_params=_params(("parallel", "arbitrary")),
        name="in_proj",
    )(x, g, w, gq, gk, lng, lnb)


def _attn_kernel(q_ref, k_ref, v_ref, o_ref, *, tq, tk):
    qi = pl.program_id(2)
    q = q_ref[...]
    jj = lax.broadcasted_iota(jnp.int32, (tk, tk), 0)
    ss = lax.broadcasted_iota(jnp.int32, (tk, tk), 1)
    upper = (jj > ss).astype(BF16)

    def block(kb, carry, acc, diagonal):
        k0 = pl.multiple_of(kb * tk, tk)
        k = k_ref[pl.ds(k0, tk), :]
        v = v_ref[pl.ds(k0, tk), :]
        z = lax.dot_general(q, k, (((1,), (1,)), ((), ())), preferred_element_type=F32)
        soft = jnp.log1p(jnp.exp(-jnp.abs(z)))
        log_beta = jnp.minimum(z, 0.0) - soft
        log_keep = log_beta - z
        if diagonal:
            t_idx = lax.broadcasted_iota(jnp.int32, (tq, tk), 0)
            s_idx = lax.broadcasted_iota(jnp.int32, (tq, tk), 1)
            past = s_idx < t_idx
            log_keep = jnp.where(past, log_keep, 0.0)
        hi = log_keep.astype(BF16)
        lo = (log_keep - hi.astype(F32)).astype(BF16)
        tail = _dot(hi, upper) + _dot(lo, upper)
        a = jnp.exp(log_beta + tail + carry)
        if diagonal:
            a = jnp.where(past, a, 0.0)
        acc = acc + _dot(a.astype(BF16), v)
        carry = carry + jnp.sum(log_keep, axis=-1, keepdims=True)
        return carry, acc

    carry0 = jnp.zeros((tq, 1), F32)
    acc0 = jnp.zeros((tq, HEAD_DIM), F32)
    carry, acc = block(qi, carry0, acc0, True)

    def body(i, state):
        return block(qi - 1 - i, state[0], state[1], False)

    carry, acc = lax.fori_loop(0, qi, body, (carry, acc))
    o_ref[...] = acc.astype(BF16)


def _attention(proj, *, bsz, seq, tq):
    m = proj.shape[0]
    nq = seq // tq
    heads = SEG // HEAD_DIM
    return pl.pallas_call(
        functools.partial(_attn_kernel, tq=tq, tk=tq),
        grid=(bsz, heads, nq),
        in_specs=[
            pl.BlockSpec((tq, HEAD_DIM), lambda b, h, i: (b * nq + i, h)),
            pl.BlockSpec((seq, HEAD_DIM), lambda b, h, i: (b, heads + h)),
            pl.BlockSpec((seq, HEAD_DIM), lambda b, h, i: (b, 2 * heads + h)),
        ],
        out_specs=pl.BlockSpec((tq, HEAD_DIM), lambda b, h, i: (b * nq + i, h)),
        out_shape=jax.ShapeDtypeStruct((m, SEG), BF16),
        compiler_params=_params(("parallel", "parallel", "arbitrary")),
        name="sb_attention",
    )(proj, proj, proj)


def _merge_kernel(o_ref, u_ref, vln_ref, ws_ref, bs_ref, woa_ref, wob_ref, ga_ref, gb_ref,
                  out_ref, s_ref, *, tm):
    j = pl.program_id(1)

    @pl.when(j == 0)
    def _():
        ii = lax.broadcasted_iota(jnp.int32, (SGU_LEN, SGU_LEN), 0) // SGU_CHUNK
        jj = lax.broadcasted_iota(jnp.int32, (SGU_LEN, SGU_LEN), 1) // SGU_CHUNK
        allowed = jj <= ii
        for g in range(SEG // HEAD_DIM):
            sl = slice(g * HEAD_DIM, (g + 1) * HEAD_DIM)
            w = jnp.where(allowed, ws_ref[g], 0.0).astype(BF16)
            bias = bs_ref[:, g:g + 1]
            for c in range(tm // SGU_LEN):
                rows = slice(c * SGU_LEN, (c + 1) * SGU_LEN)
                mixed = _dot(w, vln_ref[rows, sl]) + bias
                s_ref[rows, sl] = (u_ref[rows, sl].astype(F32) * mixed).astype(BF16)

    y_a = _dot(o_ref[...], woa_ref[...])
    y_b = _dot(s_ref[...], wob_ref[...])
    merged = ga_ref[...].astype(F32) * y_a + gb_ref[...].astype(F32) * y_b
    out_ref[...] = merged.astype(BF16)


def _merge(o, proj, ws, bs_t, woa, wob, *, tm, tn):
    m = o.shape[0]
    d = woa.shape[1]
    u_blk = 3 * SEG // SEG
    v_blk = 4 * SEG // SEG
    ga_blk = 5 * SEG // tn
    gb_blk = ga_blk + d // tn
    return pl.pallas_call(
        functools.partial(_merge_kernel, tm=tm),
        grid=(m // tm, d // tn),
        in_specs=[
            pl.BlockSpec((tm, SEG), lambda i, j: (i, 0)),
            pl.BlockSpec((tm, SEG), lambda i, j: (i, u_blk)),
            pl.BlockSpec((tm, SEG), lambda i, j: (i, v_blk)),
            pl.BlockSpec(ws.shape, lambda i, j: (0, 0, 0)),
            pl.BlockSpec(bs_t.shape, lambda i, j: (0, 0)),
            pl.BlockSpec((SEG, tn), lambda i, j: (0, j)),
            pl.BlockSpec((SEG, tn), lambda i, j: (0, j)),
            pl.BlockSpec((tm, tn), lambda i, j: (i, ga_blk + j)),
            pl.BlockSpec((tm, tn), lambda i, j: (i, gb_blk + j)),
        ],
        out_specs=pl.BlockSpec((tm, tn), lambda i, j: (i, j)),
        out_shape=jax.ShapeDtypeStruct((m, d), BF16),
        scratch_shapes=[pltpu.VMEM((tm, SEG), BF16)],
        compiler_params=_params(("parallel", "arbitrary")),
        name="sgu_merge",
    )(o, proj, proj, ws, bs_t, woa, wob, proj, proj)


def _out_proj_kernel(x_ref, mg_ref, w_ref, g_ref, xo_ref, h_ref):
    x = x_ref[...] + _dot(mg_ref[...], w_ref[...])
    xo_ref[...] = x
    ms = jnp.mean(x * x, axis=-1, keepdims=True)
    h_ref[...] = (x * lax.rsqrt(ms + EPS) * g_ref[...]).astype(BF16)


def _out_proj(x, merged, w, g, *, tm):
    m, d = x.shape
    return pl.pallas_call(
        _out_proj_kernel,
        grid=(m // tm,),
        in_specs=[
            pl.BlockSpec((tm, d), lambda i: (i, 0)),
            pl.BlockSpec((tm, d), lambda i: (i, 0)),
            pl.BlockSpec((d, d), lambda i: (0, 0)),
            pl.BlockSpec((1, d), lambda i: (0, 0)),
        ],
        out_specs=[
            pl.BlockSpec((tm, d), lambda i: (i, 0)),
            pl.BlockSpec((tm, d), lambda i: (i, 0)),
        ],
        out_shape=[jax.ShapeDtypeStruct((m, d), F32), jax.ShapeDtypeStruct((m, d), BF16)],
        compiler_params=_params(("parallel",)),
        name="out_proj",
    )(x, merged, w, g)


def _ffn_kernel(x_ref, h_ref, w1_ref, w2_ref, o_ref):
    f = pl.program_id(1)
    hid = jnp.square(jnp.maximum(_dot(h_ref[...], w1_ref[...]), 0.0)).astype(BF16)
    upd = _dot(hid, w2_ref[...])

    @pl.when(f == 0)
    def _():
        o_ref[...] = x_ref[...] + upd

    @pl.when(f > 0)
    def _():
        o_ref[...] += upd


def _ffn(x, h, w1, w2, *, tm, tf):
    m, d = x.shape
    ff = w1.shape[1]
    return pl.pallas_call(
        _ffn_kernel,
        grid=(m // tm, ff // tf),
        in_specs=[
            pl.BlockSpec((tm, d), lambda i, f: (i, 0)),
            pl.BlockSpec((tm, d), lambda i, f: (i, 0)),
            pl.BlockSpec((d, tf), lambda i, f: (0, f)),
            pl.BlockSpec((tf, d), lambda i, f: (f, 0)),
        ],
        out_specs=pl.BlockSpec((tm, d), lambda i, f: (i, 0)),
        out_shape=jax.ShapeDtypeStruct((m, d), F32),
        compiler_params=_params(("parallel", "arbitrary")),
        name="ffn",
    )(x, h, w1, w2)


def _tile(n, want):
    t = min(n, want)
    assert n % t == 0
    return t


def kernel(x, g_mix, w_in, g_q, g_k, sgu_ln_g, sgu_ln_b, w_spatial, b_spatial, w_oa, w_ob, w_out,
           g_ff, w_ff1, w_ff2):
    bsz, seq, d = x.shape
    depth = w_in.shape[0]
    m = bsz * seq
    assert seq % SGU_LEN == 0 and w_in.shape[2] == 5 * SEG + 2 * d
    xf = x.reshape(m, d)
    tq = _tile(seq, 256)
    for l in range(depth):
        proj = _in_proj(
            xf, g_mix[l].reshape(1, d), w_in[l].astype(BF16),
            g_q[l].reshape(1, SEG), g_k[l].reshape(1, SEG),
            sgu_ln_g[l].reshape(1, SEG), sgu_ln_b[l].reshape(1, SEG), tm=_tile(m, 1024))
        o = _attention(proj, bsz=bsz, seq=seq, tq=tq)
        merged = _merge(o, proj, w_spatial[l], b_spatial[l].T, w_oa[l].astype(BF16),
                        w_ob[l].astype(BF16), tm=_tile(m, 1024), tn=512)
        xf, h2 = _out_proj(xf, merged, w_out[l].astype(BF16), g_ff[l].reshape(1, d),
                           tm=_tile(m, 512))
        xf = _ffn(xf, h2, w_ff1[l].astype(BF16), w_ff2[l].astype(BF16), tm=_tile(m, 512), tf=512)
    return xf.reshape(bsz, seq, d)
```

```python
import functools

import jax
import jax.numpy as jnp
from jax import lax
from jax.experimental import pallas as pl
from jax.experimental.pallas import tpu as pltpu

EPS = 1e-6
HEAD_DIM = 128
SGU_LEN = 128
SGU_CHUNK = 64
SEG = 1024
MXU_WIDTH = 256

VMEM_LIMIT_BYTES = 56 * 1024 * 1024

BF16 = jnp.bfloat16
F32 = jnp.float32

LOG2_E = 1.4426950408889634
STICK_ZERO_LOG2 = -160.0


def _params(sem):
    return pltpu.CompilerParams(dimension_semantics=sem, vmem_limit_bytes=VMEM_LIMIT_BYTES)


def _dot(a, b):
    return jnp.dot(a, b, preferred_element_type=F32)


def _gelu(x):
    return 0.5 * x * (1.0 + lax.erf(x * (2.0 ** -0.5)))


def _in_proj_kernel(x_ref, g_ref, w_ref, gq_ref, gk_ref, lng_ref, lnb_ref, o_ref, h_ref, *, scale):
    j = pl.program_id(1)
    chunks = [slice(c, c + MXU_WIDTH) for c in range(0, SEG, MXU_WIDTH)]

    @pl.when(j == 0)
    def _():
        x = x_ref[...]
        ms = jnp.mean(x * x, axis=-1, keepdims=True)
        h_ref[...] = (x * lax.rsqrt(ms + EPS) * g_ref[...]).astype(BF16)

    def chunk_dot(sl):
        return _dot(h_ref[...], w_ref[:, sl])

    def head_norm(gain_ref, mul):
        for sl in chunks:
            acc = chunk_dot(sl)
            for h in range(0, MXU_WIDTH, HEAD_DIM):
                blk = acc[:, h:h + HEAD_DIM]
                cols = slice(sl.start + h, sl.start + h + HEAD_DIM)
                ms = jnp.mean(blk * blk, axis=-1, keepdims=True)
                o_ref[:, cols] = (blk * (lax.rsqrt(ms + EPS) * mul) * gain_ref[:, cols]).astype(BF16)

    @pl.when(j == 0)
    def _():
        head_norm(gq_ref, scale)

    @pl.when(j == 1)
    def _():
        head_norm(gk_ref, 1.0)

    @pl.when(j == 2)
    def _():
        for sl in chunks:
            o_ref[:, sl] = chunk_dot(sl).astype(BF16)

    @pl.when(j == 3)
    def _():
        for sl in chunks:
            o_ref[:, sl] = _gelu(chunk_dot(sl)).astype(BF16)

    @pl.when(j == 4)
    def _():
        vs = [_gelu(chunk_dot(sl)) for sl in chunks]
        mu = sum(jnp.sum(v, axis=-1, keepdims=True) for v in vs) * (1.0 / SEG)
        vs = [v - mu for v in vs]
        var = sum(jnp.sum(v * v, axis=-1, keepdims=True) for v in vs) * (1.0 / SEG)
        inv = lax.rsqrt(var + EPS)
        for sl, v in zip(chunks, vs):
            o_ref[:, sl] = (v * inv * lng_ref[:, sl] + lnb_ref[:, sl]).astype(BF16)

    @pl.when(j >= 5)
    def _():
        for sl in chunks:
            o_ref[:, sl] = jax.nn.sigmoid(chunk_dot(sl)).astype(BF16)


def _in_proj(x, g, w, gq, gk, lng, lnb, *, tm):
    m, d = x.shape
    n = w.shape[1]
    assert n % SEG == 0 and m % tm == 0
    row = lambda i, j: (0, 0)
    return pl.pallas_call(
        functools.partial(_in_proj_kernel, scale=HEAD_DIM ** -0.5 * LOG2_E),
        grid=(m // tm, n // SEG),
        in_specs=[
            pl.BlockSpec((tm, d), lambda i, j: (i, 0)),
            pl.BlockSpec((1, d), row),
            pl.BlockSpec((d, SEG), lambda i, j: (0, j)),
            pl.BlockSpec((1, SEG), row),
            pl.BlockSpec((1, SEG), row),
            pl.BlockSpec((1, SEG), row),
            pl.BlockSpec((1, SEG), row),
        ],
        out_specs=pl.BlockSpec((tm, SEG), lambda i, j: (i, j)),
        out_shape=jax.ShapeDtypeStruct((m, n), BF16),
        scratch_shapes=[pltpu.VMEM((tm, d), BF16)],
        compiler_params=_params(("parallel", "arbitrary")),
        name="in_proj",
    )(x, g, w, gq, gk, lng, lnb)


def _attn_kernel(q_ref, k_ref, v_ref, o_ref, *, tq):
    tk = tq
    qi = pl.program_id(2)
    q = q_ref[...]
    jj = lax.broadcasted_iota(jnp.int32, (tk, tk), 0)
    ss = lax.broadcasted_iota(jnp.int32, (tk, tk), 1)
    upper = (jj > ss).astype(BF16)
    upper2 = jnp.concatenate([upper, upper], axis=0)

    def logits(kb):
        k0 = pl.multiple_of(kb * tk, tk)
        z = lax.dot_general(q, k_ref[pl.ds(k0, tk), :], (((1,), (1,)), ((), ())),
                            preferred_element_type=F32)
        soft = jnp.log(1.0 + jnp.exp2(-jnp.abs(z))) * LOG2_E
        log_beta = jnp.minimum(z, 0.0) - soft
        log_keep = log_beta - z
        return log_beta, log_keep, v_ref[pl.ds(k0, tk), :]

    def weights(log_beta, log_keep, carry):
        hi = log_keep.astype(BF16)
        lo = (log_keep - hi.astype(F32)).astype(BF16)
        tail = _dot(jnp.concatenate([hi, lo], axis=1), upper2)
        return jnp.exp2(log_beta + tail + carry)

    log_beta, log_keep, v = logits(qi)
    past = ss < jj
    log_keep = jnp.where(past, log_keep, 0.0)
    a = jnp.where(past, weights(log_beta, log_keep, 0.0), 0.0)
    acc = _dot(a.astype(BF16), v)
    carry = jnp.sum(log_keep, axis=-1, keepdims=True)

    has_prev = (qi > 0).astype(F32)
    log_beta, log_keep, v = logits(jnp.maximum(qi - 1, 0))
    a = weights(log_beta, log_keep, carry)
    acc = acc + has_prev * _dot(a.astype(BF16), v)
    carry = carry + jnp.sum(log_keep, axis=-1, keepdims=True)

    def more(state):
        kb, carry, _ = state
        return jnp.logical_and(kb >= 0, jnp.max(carry) > STICK_ZERO_LOG2)

    def step(state):
        kb, carry, acc = state
        log_beta, log_keep, v = logits(kb)
        a = weights(log_beta, log_keep, carry)
        acc = acc + _dot(a.astype(BF16), v)
        carry = carry + jnp.sum(log_keep, axis=-1, keepdims=True)
        return kb - 1, carry, acc

    _, _, acc = lax.while_loop(more, step, (qi - 2, carry, acc))
    o_ref[...] = acc.astype(BF16)


def _attention(proj, *, bsz, seq, tq):
    m = proj.shape[0]
    nq = seq // tq
    heads = SEG // HEAD_DIM
    return pl.pallas_call(
        functools.partial(_attn_kernel, tq=tq),
        grid=(bsz, heads, nq),
        in_specs=[
            pl.BlockSpec((tq, HEAD_DIM), lambda b, h, i: (b * nq + i, h)),
            pl.BlockSpec((seq, HEAD_DIM), lambda b, h, i: (b, heads + h)),
            pl.BlockSpec((seq, HEAD_DIM), lambda b, h, i: (b, 2 * heads + h)),
        ],
        out_specs=pl.BlockSpec((tq, HEAD_DIM), lambda b, h, i: (b * nq + i, h)),
        out_shape=jax.ShapeDtypeStruct((m, SEG), BF16),
        compiler_params=_params(("parallel", "parallel", "arbitrary")),
        name="sb_attention",
    )(proj, proj, proj)


def _merge_kernel(o_ref, u_ref, vln_ref, ws_ref, bs_ref, woa_ref, wob_ref, ga_ref, gb_ref,
                  out_ref, s_ref, *, tm):
    j = pl.program_id(1)

    @pl.when(j == 0)
    def _():
        ii = lax.broadcasted_iota(jnp.int32, (SGU_LEN, SGU_LEN), 0) // SGU_CHUNK
        jj = lax.broadcasted_iota(jnp.int32, (SGU_LEN, SGU_LEN), 1) // SGU_CHUNK
        allowed = jj <= ii
        for g in range(SEG // HEAD_DIM):
            sl = slice(g * HEAD_DIM, (g + 1) * HEAD_DIM)
            w = jnp.where(allowed, ws_ref[g], 0.0).astype(BF16)
            bias = bs_ref[:, g:g + 1]
            for c in range(tm // SGU_LEN):
                rows = slice(c * SGU_LEN, (c + 1) * SGU_LEN)
                mixed = _dot(w, vln_ref[rows, sl]) + bias
                s_ref[rows, sl] = (u_ref[rows, sl].astype(F32) * mixed).astype(BF16)

    y_a = _dot(o_ref[...], woa_ref[...])
    y_b = _dot(s_ref[...], wob_ref[...])
    merged = ga_ref[...].astype(F32) * y_a + gb_ref[...].astype(F32) * y_b
    out_ref[...] = merged.astype(BF16)


def _merge(o, proj, ws, bs_t, woa, wob, *, tm, tn):
    m = o.shape[0]
    d = woa.shape[1]
    u_blk = 3 * SEG // SEG
    v_blk = 4 * SEG // SEG
    ga_blk = 5 * SEG // tn
    gb_blk = ga_blk + d // tn
    return pl.pallas_call(
        functools.partial(_merge_kernel, tm=tm),
        grid=(m // tm, d // tn),
        in_specs=[
            pl.BlockSpec((tm, SEG), lambda i, j: (i, 0)),
            pl.BlockSpec((tm, SEG), lambda i, j: (i, u_blk)),
            pl.BlockSpec((tm, SEG), lambda i, j: (i, v_blk)),
            pl.BlockSpec(ws.shape, lambda i, j: (0, 0, 0)),
            pl.BlockSpec(bs_t.shape, lambda i, j: (0, 0)),
            pl.BlockSpec((SEG, tn), lambda i, j: (0, j)),
            pl.BlockSpec((SEG, tn), lambda i, j: (0, j)),
            pl.BlockSpec((tm, tn), lambda i, j: (i, ga_blk + j)),
            pl.BlockSpec((tm, tn), lambda i, j: (i, gb_blk + j)),
        ],
        out_specs=pl.BlockSpec((tm, tn), lambda i, j: (i, j)),
        out_shape=jax.ShapeDtypeStruct((m, d), BF16),
        scratch_shapes=[pltpu.VMEM((tm, SEG), BF16)],
        compiler_params=_params(("parallel", "arbitrary")),
        name="sgu_merge",
    )(o, proj, proj, ws, bs_t, woa, wob, proj, proj)


def _out_proj_kernel(x_ref, mg_ref, w_ref, g_ref, xo_ref, h_ref):
    x = x_ref[...] + _dot(mg_ref[...], w_ref[...])
    xo_ref[...] = x
    ms = jnp.mean(x * x, axis=-1, keepdims=True)
    h_ref[...] = (x * lax.rsqrt(ms + EPS) * g_ref[...]).astype(BF16)


def _out_proj(x, merged, w, g, *, tm):
    m, d = x.shape
    return pl.pallas_call(
        _out_proj_kernel,
        grid=(m // tm,),
        in_specs=[
            pl.BlockSpec((tm, d), lambda i: (i, 0)),
            pl.BlockSpec((tm, d), lambda i: (i, 0)),
            pl.BlockSpec((d, d), lambda i: (0, 0)),
            pl.BlockSpec((1, d), lambda i: (0, 0)),
        ],
        out_specs=[
            pl.BlockSpec((tm, d), lambda i: (i, 0)),
            pl.BlockSpec((tm, d), lambda i: (i, 0)),
        ],
        out_shape=[jax.ShapeDtypeStruct((m, d), F32), jax.ShapeDtypeStruct((m, d), BF16)],
        compiler_params=_params(("parallel",)),
        name="out_proj",
    )(x, merged, w, g)


def _ffn_kernel(x_ref, h_ref, w1_ref, w2_ref, o_ref):
    f = pl.program_id(1)
    hid = jnp.square(jnp.maximum(_dot(h_ref[...], w1_ref[...]), 0.0)).astype(BF16)
    upd = _dot(hid, w2_ref[...])

    @pl.when(f == 0)
    def _():
        o_ref[...] = x_ref[...] + upd

    @pl.when(f > 0)
    def _():
        o_ref[...] += upd


def _ffn(x, h, w1, w2, *, tm, tf):
    m, d = x.shape
    ff = w1.shape[1]
    return pl.pallas_call(
        _ffn_kernel,
        grid=(m // tm, ff // tf),
        in_specs=[
            pl.BlockSpec((tm, d), lambda i, f: (i, 0)),
            pl.BlockSpec((tm, d), lambda i, f: (i, 0)),
            pl.BlockSpec((d, tf), lambda i, f: (0, f)),
            pl.BlockSpec((tf, d), lambda i, f: (f, 0)),
        ],
        out_specs=pl.BlockSpec((tm, d), lambda i, f: (i, 0)),
        out_shape=jax.ShapeDtypeStruct((m, d), F32),
        compiler_params=_params(("parallel", "arbitrary")),
        name="ffn",
    )(x, h, w1, w2)


def _tile(n, want):
    t = min(n, want)
    assert n % t == 0
    return t


def kernel(x, g_mix, w_in, g_q, g_k, sgu_ln_g, sgu_ln_b, w_spatial, b_spatial, w_oa, w_ob, w_out,
           g_ff, w_ff1, w_ff2):
    bsz, seq, d = x.shape
    depth = w_in.shape[0]
    m = bsz * seq
    assert seq % SGU_LEN == 0 and w_in.shape[2] == 5 * SEG + 2 * d
    xf = x.reshape(m, d)
    tq = _tile(seq, 256)
    for l in range(depth):
        proj = _in_proj(
            xf, g_mix[l].reshape(1, d), w_in[l].astype(BF16),
            g_q[l].reshape(1, SEG), g_k[l].reshape(1, SEG),
            sgu_ln_g[l].reshape(1, SEG), sgu_ln_b[l].reshape(1, SEG), tm=_tile(m, 1024))
        o = _attention(proj, bsz=bsz, seq=seq, tq=tq)
        merged = _merge(o, proj, w_spatial[l], b_spatial[l].T, w_oa[l].astype(BF16),
                        w_ob[l].astype(BF16), tm=_tile(m, 1024), tn=512)
        xf, h2 = _out_proj(xf, merged, w_out[l].astype(BF16), g_ff[l].reshape(1, d),
                           tm=_tile(m, 512))
        xf = _ffn(xf, h2, w_ff1[l].astype(BF16), w_ff2[l].astype(BF16), tm=_tile(m, 512), tf=1024)
    return xf.reshape(bsz, seq, d)
```

```python
import functools

import jax
import jax.numpy as jnp
from jax import lax
from jax.experimental import pallas as pl
from jax.experimental.pallas import tpu as pltpu

EPS = 1e-6
HEAD_DIM = 128
SGU_LEN = 128
SGU_CHUNK = 64
SEG = 1024
MXU_WIDTH = 256

VMEM_LIMIT_BYTES = 56 * 1024 * 1024

BF16 = jnp.bfloat16
F32 = jnp.float32

LOG2_E = 1.4426950408889634
STICK_ZERO_LOG2 = -160.0


def _params(sem):
    return pltpu.CompilerParams(dimension_semantics=sem, vmem_limit_bytes=VMEM_LIMIT_BYTES)


def _dot(a, b):
    return jnp.dot(a, b, preferred_element_type=F32)


def _gelu(x):
    return 0.5 * x * (1.0 + lax.erf(x * (2.0 ** -0.5)))


def _in_proj_kernel(x_ref, g_ref, w_ref, gq_ref, gk_ref, lng_ref, lnb_ref, o_ref, h_ref, *, scale):
    j = pl.program_id(1)
    chunks = [slice(c, c + MXU_WIDTH) for c in range(0, SEG, MXU_WIDTH)]

    @pl.when(j == 0)
    def _():
        x = x_ref[...]
        ms = jnp.mean(x * x, axis=-1, keepdims=True)
        h_ref[...] = (x * lax.rsqrt(ms + EPS) * g_ref[...]).astype(BF16)

    def chunk_dot(sl):
        return _dot(h_ref[...], w_ref[:, sl])

    def head_norm(gain_ref, mul):
        for sl in chunks:
            acc = chunk_dot(sl)
            for h in range(0, MXU_WIDTH, HEAD_DIM):
                blk = acc[:, h:h + HEAD_DIM]
                cols = slice(sl.start + h, sl.start + h + HEAD_DIM)
                ms = jnp.mean(blk * blk, axis=-1, keepdims=True)
                o_ref[:, cols] = (blk * (lax.rsqrt(ms + EPS) * mul) * gain_ref[:, cols]).astype(BF16)

    @pl.when(j == 0)
    def _():
        head_norm(gq_ref, scale)

    @pl.when(j == 1)
    def _():
        head_norm(gk_ref, 1.0)

    @pl.when(j == 2)
    def _():
        for sl in chunks:
            o_ref[:, sl] = chunk_dot(sl).astype(BF16)

    @pl.when(j == 3)
    def _():
        for sl in chunks:
            o_ref[:, sl] = _gelu(chunk_dot(sl)).astype(BF16)

    @pl.when(j == 4)
    def _():
        vs = [_gelu(chunk_dot(sl)) for sl in chunks]
        mu = sum(jnp.sum(v, axis=-1, keepdims=True) for v in vs) * (1.0 / SEG)
        vs = [v - mu for v in vs]
        var = sum(jnp.sum(v * v, axis=-1, keepdims=True) for v in vs) * (1.0 / SEG)
        inv = lax.rsqrt(var + EPS)
        for sl, v in zip(chunks, vs):
            o_ref[:, sl] = (v * inv * lng_ref[:, sl] + lnb_ref[:, sl]).astype(BF16)

    @pl.when(j >= 5)
    def _():
        for sl in chunks:
            o_ref[:, sl] = jax.nn.sigmoid(chunk_dot(sl)).astype(BF16)


def _in_proj(x, g, w, gq, gk, lng, lnb, *, layer, tm):
    m, d = x.shape
    n = w.shape[2]
    assert n % SEG == 0 and m % tm == 0
    vec = lambda width: pl.BlockSpec((None, 1, width), lambda i, j: (layer, 0, 0))
    return pl.pallas_call(
        functools.partial(_in_proj_kernel, scale=HEAD_DIM ** -0.5 * LOG2_E),
        grid=(m // tm, n // SEG),
        in_specs=[
            pl.BlockSpec((tm, d), lambda i, j: (i, 0)),
            vec(d),
            pl.BlockSpec((None, d, SEG), lambda i, j: (layer, 0, j)),
            vec(SEG), vec(SEG), vec(SEG), vec(SEG),
        ],
        out_specs=pl.BlockSpec((tm, SEG), lambda i, j: (i, j)),
        out_shape=jax.ShapeDtypeStruct((m, n), BF16),
        scratch_shapes=[pltpu.VMEM((tm, d), BF16)],
        compiler_params=_params(("parallel", "arbitrary")),
        name="in_proj",
    )(x, g, w, gq, gk, lng, lnb)


def _attn_kernel(q_ref, k_ref, v_ref, o_ref, *, tq, n_heads):
    tk = tq
    qi = pl.program_id(2)
    jj = lax.broadcasted_iota(jnp.int32, (tk, tk), 0)
    ss = lax.broadcasted_iota(jnp.int32, (tk, tk), 1)
    upper = (jj > ss).astype(BF16)
    upper2 = jnp.concatenate([upper, upper], axis=0)
    past = ss < jj

    def logits(cols, kb):
        k0 = pl.multiple_of(kb * tk, tk)
        z = lax.dot_general(q_ref[:, cols], k_ref[pl.ds(k0, tk), cols], (((1,), (1,)), ((), ())),
                            preferred_element_type=F32)
        soft = jnp.log(1.0 + jnp.exp2(-jnp.abs(z))) * LOG2_E
        log_beta = jnp.minimum(z, 0.0) - soft
        log_keep = log_beta - z
        return log_beta, log_keep, v_ref[pl.ds(k0, tk), cols]

    def weights(log_beta, log_keep, carry):
        hi = log_keep.astype(BF16)
        lo = (log_keep - hi.astype(F32)).astype(BF16)
        tail = _dot(jnp.concatenate([hi, lo], axis=1), upper2)
        return jnp.exp2(log_beta + tail + carry)

    has_prev = (qi > 0).astype(F32)
    heads = [slice(h * HEAD_DIM, (h + 1) * HEAD_DIM) for h in range(n_heads)]
    states = []
    for cols in heads:
        log_beta, log_keep, v = logits(cols, qi)
        log_keep = jnp.where(past, log_keep, 0.0)
        a = jnp.where(past, weights(log_beta, log_keep, 0.0), 0.0)
        acc = _dot(a.astype(BF16), v)
        carry = jnp.sum(log_keep, axis=-1, keepdims=True)
        log_beta, log_keep, v = logits(cols, jnp.maximum(qi - 1, 0))
        a = weights(log_beta, log_keep, carry)
        acc = acc + has_prev * _dot(a.astype(BF16), v)
        carry = carry + jnp.sum(log_keep, axis=-1, keepdims=True)
        states.append((carry, acc))

    def more(state):
        kb, carry, _ = state
        return jnp.logical_and(kb >= 0, jnp.max(carry) > STICK_ZERO_LOG2)

    for cols, (carry, acc) in zip(heads, states):
        def step(state, cols=cols):
            kb, carry, acc = state
            log_beta, log_keep, v = logits(cols, kb)
            a = weights(log_beta, log_keep, carry)
            acc = acc + _dot(a.astype(BF16), v)
            carry = carry + jnp.sum(log_keep, axis=-1, keepdims=True)
            return kb - 1, carry, acc

        _, _, acc = lax.while_loop(more, step, (qi - 2, carry, acc))
        o_ref[:, cols] = acc.astype(BF16)


def _attention(proj, *, bsz, seq, tq, n_heads):
    m = proj.shape[0]
    nq = seq // tq
    width = n_heads * HEAD_DIM
    groups = SEG // width
    return pl.pallas_call(
        functools.partial(_attn_kernel, tq=tq, n_heads=n_heads),
        grid=(bsz, groups, nq),
        in_specs=[
            pl.BlockSpec((tq, width), lambda b, h, i: (b * nq + i, h)),
            pl.BlockSpec((seq, width), lambda b, h, i: (b, groups + h)),
            pl.BlockSpec((seq, width), lambda b, h, i: (b, 2 * groups + h)),
        ],
        out_specs=pl.BlockSpec((tq, width), lambda b, h, i: (b * nq + i, h)),
        out_shape=jax.ShapeDtypeStruct((m, SEG), BF16),
        compiler_params=_params(("parallel", "parallel", "arbitrary")),
        name="sb_attention",
    )(proj, proj, proj)


def _merge_kernel(o_ref, u_ref, vln_ref, ws_ref, bs_ref, woa_ref, wob_ref, ga_ref, gb_ref,
                  out_ref, s_ref, *, tm):
    j = pl.program_id(1)

    def spatial_gating():
        ii = lax.broadcasted_iota(jnp.int32, (SGU_LEN, SGU_LEN), 0) // SGU_CHUNK
        jj = lax.broadcasted_iota(jnp.int32, (SGU_LEN, SGU_LEN), 1) // SGU_CHUNK
        allowed = jj <= ii
        for g in range(SEG // HEAD_DIM):
            sl = slice(g * HEAD_DIM, (g + 1) * HEAD_DIM)
            w = jnp.where(allowed, ws_ref[g], 0.0).astype(BF16)
            bias = bs_ref[:, g:g + 1]
            for c in range(tm // SGU_LEN):
                rows = slice(c * SGU_LEN, (c + 1) * SGU_LEN)
                mixed = _dot(w, vln_ref[rows, sl]) + bias
                s_ref[rows, sl] = (u_ref[rows, sl].astype(F32) * mixed).astype(BF16)

    def project():
        y_a = _dot(o_ref[...], woa_ref[...])
        y_b = _dot(s_ref[...], wob_ref[...])
        merged = ga_ref[...].astype(F32) * y_a + gb_ref[...].astype(F32) * y_b
        out_ref[...] = merged.astype(BF16)

    @pl.when(j == 0)
    def _():
        spatial_gating()
        project()

    @pl.when(j > 0)
    def _():
        project()


def _merge(o, proj, ws, bs_t, woa, wob, *, layer, tm, tn):
    m = o.shape[0]
    d = woa.shape[2]
    u_blk = 3
    v_blk = 4
    ga_blk = 5 * SEG // tn
    gb_blk = ga_blk + d // tn
    return pl.pallas_call(
        functools.partial(_merge_kernel, tm=tm),
        grid=(m // tm, d // tn),
        in_specs=[
            pl.BlockSpec((tm, SEG), lambda i, j: (i, 0)),
            pl.BlockSpec((tm, SEG), lambda i, j: (i, u_blk)),
            pl.BlockSpec((tm, SEG), lambda i, j: (i, v_blk)),
            pl.BlockSpec((None,) + ws.shape[1:], lambda i, j: (layer, 0, 0, 0)),
            pl.BlockSpec((None,) + bs_t.shape[1:], lambda i, j: (layer, 0, 0)),
            pl.BlockSpec((None, SEG, tn), lambda i, j: (layer, 0, j)),
            pl.BlockSpec((None, SEG, tn), lambda i, j: (layer, 0, j)),
            pl.BlockSpec((tm, tn), lambda i, j: (i, ga_blk + j)),
            pl.BlockSpec((tm, tn), lambda i, j: (i, gb_blk + j)),
        ],
        out_specs=pl.BlockSpec((tm, tn), lambda i, j: (i, j)),
        out_shape=jax.ShapeDtypeStruct((m, d), BF16),
        scratch_shapes=[pltpu.VMEM((tm, SEG), BF16)],
        compiler_params=_params(("parallel", "arbitrary")),
        name="sgu_merge",
    )(o, proj, proj, ws, bs_t, woa, wob, proj, proj)


def _out_proj_kernel(x_ref, mg_ref, w_ref, g_ref, xo_ref, h_ref):
    x = x_ref[...] + _dot(mg_ref[...], w_ref[...])
    xo_ref[...] = x
    ms = jnp.mean(x * x, axis=-1, keepdims=True)
    h_ref[...] = (x * lax.rsqrt(ms + EPS) * g_ref[...]).astype(BF16)


def _out_proj(x, merged, w, g, *, layer, tm):
    m, d = x.shape
    return pl.pallas_call(
        _out_proj_kernel,
        grid=(m // tm,),
        in_specs=[
            pl.BlockSpec((tm, d), lambda i: (i, 0)),
            pl.BlockSpec((tm, d), lambda i: (i, 0)),
            pl.BlockSpec((None, d, d), lambda i: (layer, 0, 0)),
            pl.BlockSpec((None, 1, d), lambda i: (layer, 0, 0)),
        ],
        out_specs=[
            pl.BlockSpec((tm, d), lambda i: (i, 0)),
            pl.BlockSpec((tm, d), lambda i: (i, 0)),
        ],
        out_shape=[jax.ShapeDtypeStruct((m, d), F32), jax.ShapeDtypeStruct((m, d), BF16)],
        compiler_params=_params(("parallel",)),
        name="out_proj",
    )(x, merged, w, g)


def _ffn_kernel(x_ref, h_ref, w1_ref, w2_ref, o_ref, hid_ref, *, n_up):
    s = pl.program_id(1)

    @pl.when(s < n_up)
    def _():
        hid_ref[s] = jnp.square(jnp.maximum(_dot(h_ref[...], w1_ref[...]), 0.0)).astype(BF16)

    @pl.when(s >= n_up)
    def _():
        upd = _dot(hid_ref[0], w2_ref[0])
        for c in range(1, n_up):
            upd = upd + _dot(hid_ref[c], w2_ref[c])
        o_ref[...] = x_ref[...] + upd


def _ffn(x, h, w1, w2, *, layer, tm, tf, tn):
    m, d = x.shape
    ff = w1.shape[2]
    n_up, n_down = ff // tf, d // tn
    down = lambda s: jnp.maximum(s - n_up, 0)
    w2 = w2.reshape(w2.shape[0], n_up, tf, d)
    return pl.pallas_call(
        functools.partial(_ffn_kernel, n_up=n_up),
        grid=(m // tm, n_up + n_down),
        in_specs=[
            pl.BlockSpec((tm, tn), lambda i, s: (i, down(s))),
            pl.BlockSpec((tm, d), lambda i, s: (i, 0)),
            pl.BlockSpec((None, d, tf), lambda i, s: (layer, 0, jnp.minimum(s, n_up - 1))),
            pl.BlockSpec((None, n_up, tf, tn), lambda i, s: (layer, 0, 0, down(s))),
        ],
        out_specs=pl.BlockSpec((tm, tn), lambda i, s: (i, down(s))),
        out_shape=jax.ShapeDtypeStruct((m, d), F32),
        scratch_shapes=[pltpu.VMEM((n_up, tm, tf), BF16)],
        compiler_params=_params(("parallel", "arbitrary")),
        name="ffn",
    )(x, h, w1, w2)


def _tile(n, want):
    t = min(n, want)
    assert n % t == 0
    return t


def kernel(x, g_mix, w_in, g_q, g_k, sgu_ln_g, sgu_ln_b, w_spatial, b_spatial, w_oa, w_ob, w_out,
           g_ff, w_ff1, w_ff2):
    bsz, seq, d = x.shape
    depth = w_in.shape[0]
    m = bsz * seq
    assert seq % SGU_LEN == 0 and w_in.shape[2] == 5 * SEG + 2 * d
    xf = x.reshape(m, d)
    w_in, w_oa, w_ob, w_out, w_ff1, w_ff2 = (
        w.astype(BF16) for w in (w_in, w_oa, w_ob, w_out, w_ff1, w_ff2))
    g_mix, g_ff = g_mix.reshape(depth, 1, d), g_ff.reshape(depth, 1, d)
    g_q, g_k, sgu_ln_g, sgu_ln_b = (
        p.reshape(depth, 1, SEG) for p in (g_q, g_k, sgu_ln_g, sgu_ln_b))
    b_spatial_t = b_spatial.transpose(0, 2, 1)
    for l in range(depth):
        proj = _in_proj(xf, g_mix, w_in, g_q, g_k, sgu_ln_g, sgu_ln_b, layer=l, tm=_tile(m, 1024))
        o = _attention(proj, bsz=bsz, seq=seq, tq=_tile(seq, 256), n_heads=2)
        merged = _merge(o, proj, w_spatial, b_spatial_t, w_oa, w_ob, layer=l,
                        tm=_tile(m, 1024), tn=512)
        xf, h2 = _out_proj(xf, merged, w_out, g_ff, layer=l, tm=_tile(m, 512))
        xf = _ffn(xf, h2, w_ff1, w_ff2, layer=l, tm=_tile(m, 512), tf=1024, tn=512)
    return xf.reshape(bsz, seq, d)
```

```python
import functools

import jax
import jax.numpy as jnp
from jax import lax
from jax.experimental import pallas as pl
from jax.experimental.pallas import tpu as pltpu

EPS = 1e-6
HEAD_DIM = 128
SGU_LEN = 128
SGU_CHUNK = 64
SEG = 1024
MXU_WIDTH = 256
FFN_UP_TILE = 1024
FFN_DOWN_TILE = 256

VMEM_LIMIT_BYTES = 56 * 1024 * 1024

BF16 = jnp.bfloat16
F32 = jnp.float32

LOG2_E = 1.4426950408889634
STICK_ZERO_LOG2 = -160.0


def _params(sem):
    return pltpu.CompilerParams(dimension_semantics=sem, vmem_limit_bytes=VMEM_LIMIT_BYTES)


def _dot(a, b):
    return jnp.dot(a, b, preferred_element_type=F32)


def _gelu(x):
    return 0.5 * x * (1.0 + lax.erf(x * (2.0 ** -0.5)))


def _in_proj_kernel(x_ref, g_ref, w_ref, gq_ref, gk_ref, lng_ref, lnb_ref, o_ref, h_ref, *, scale):
    j = pl.program_id(1)
    chunks = [slice(c, c + MXU_WIDTH) for c in range(0, SEG, MXU_WIDTH)]

    @pl.when(j == 0)
    def _():
        x = x_ref[...]
        ms = jnp.mean(x * x, axis=-1, keepdims=True)
        h_ref[...] = (x * lax.rsqrt(ms + EPS) * g_ref[...]).astype(BF16)

    def chunk_dot(sl):
        return _dot(h_ref[...], w_ref[:, sl])

    def head_norm(gain_ref, mul):
        for sl in chunks:
            acc = chunk_dot(sl)
            for h in range(0, MXU_WIDTH, HEAD_DIM):
                blk = acc[:, h:h + HEAD_DIM]
                cols = slice(sl.start + h, sl.start + h + HEAD_DIM)
                ms = jnp.mean(blk * blk, axis=-1, keepdims=True)
                o_ref[:, cols] = (blk * (lax.rsqrt(ms + EPS) * mul) * gain_ref[:, cols]).astype(BF16)

    @pl.when(j == 0)
    def _():
        head_norm(gq_ref, scale)

    @pl.when(j == 1)
    def _():
        head_norm(gk_ref, 1.0)

    @pl.when(j == 2)
    def _():
        for sl in chunks:
            o_ref[:, sl] = chunk_dot(sl).astype(BF16)

    @pl.when(j == 3)
    def _():
        for sl in chunks:
            o_ref[:, sl] = _gelu(chunk_dot(sl)).astype(BF16)

    @pl.when(j == 4)
    def _():
        vs = [_gelu(chunk_dot(sl)) for sl in chunks]
        mu = sum(jnp.sum(v, axis=-1, keepdims=True) for v in vs) * (1.0 / SEG)
        vs = [v - mu for v in vs]
        var = sum(jnp.sum(v * v, axis=-1, keepdims=True) for v in vs) * (1.0 / SEG)
        inv = lax.rsqrt(var + EPS)
        for sl, v in zip(chunks, vs):
            o_ref[:, sl] = (v * inv * lng_ref[:, sl] + lnb_ref[:, sl]).astype(BF16)

    @pl.when(j >= 5)
    def _():
        for sl in chunks:
            o_ref[:, sl] = jax.nn.sigmoid(chunk_dot(sl)).astype(BF16)


def _in_proj(x, g, w, gq, gk, lng, lnb, *, layer, tm):
    m, d = x.shape
    n_tiles = w.shape[1]
    assert m % tm == 0
    vec = lambda width: pl.BlockSpec((None, 1, width), lambda i, j: (layer, 0, 0))
    return pl.pallas_call(
        functools.partial(_in_proj_kernel, scale=HEAD_DIM ** -0.5 * LOG2_E),
        grid=(m // tm, n_tiles),
        in_specs=[
            pl.BlockSpec((tm, d), lambda i, j: (i, 0)),
            vec(d),
            pl.BlockSpec((None, None, d, SEG), lambda i, j: (layer, j, 0, 0)),
            vec(SEG), vec(SEG), vec(SEG), vec(SEG),
        ],
        out_specs=pl.BlockSpec((tm, SEG), lambda i, j: (i, j)),
        out_shape=jax.ShapeDtypeStruct((m, n_tiles * SEG), BF16),
        scratch_shapes=[pltpu.VMEM((tm, d), BF16)],
        compiler_params=_params(("parallel", "arbitrary")),
        name="in_proj",
    )(x, g, w, gq, gk, lng, lnb)


def _attn_kernel(q_ref, k_ref, v_ref, o_ref, *, tq, n_heads):
    tk = tq
    qi = pl.program_id(2)
    jj = lax.broadcasted_iota(jnp.int32, (tk, tk), 0)
    ss = lax.broadcasted_iota(jnp.int32, (tk, tk), 1)
    upper = (jj > ss).astype(BF16)
    upper2 = jnp.concatenate([upper, upper], axis=0)
    past = ss < jj

    def logits(cols, kb):
        k0 = pl.multiple_of(kb * tk, tk)
        z = lax.dot_general(q_ref[:, cols], k_ref[pl.ds(k0, tk), cols], (((1,), (1,)), ((), ())),
                            preferred_element_type=F32)
        soft = jnp.log(1.0 + jnp.exp2(-jnp.abs(z))) * LOG2_E
        log_beta = jnp.minimum(z, 0.0) - soft
        log_keep = log_beta - z
        return log_beta, log_keep, v_ref[pl.ds(k0, tk), cols]

    def weights(log_beta, log_keep, carry):
        hi = log_keep.astype(BF16)
        lo = (log_keep - hi.astype(F32)).astype(BF16)
        tail = _dot(jnp.concatenate([hi, lo], axis=1), upper2)
        return jnp.exp2(log_beta + tail + carry)

    has_prev = (qi > 0).astype(F32)
    heads = [slice(h * HEAD_DIM, (h + 1) * HEAD_DIM) for h in range(n_heads)]
    states = []
    for cols in heads:
        log_beta, log_keep, v = logits(cols, qi)
        log_keep = jnp.where(past, log_keep, 0.0)
        a = jnp.where(past, weights(log_beta, log_keep, 0.0), 0.0)
        acc = _dot(a.astype(BF16), v)
        carry = jnp.sum(log_keep, axis=-1, keepdims=True)
        log_beta, log_keep, v = logits(cols, jnp.maximum(qi - 1, 0))
        a = weights(log_beta, log_keep, carry)
        acc = acc + has_prev * _dot(a.astype(BF16), v)
        carry = carry + jnp.sum(log_keep, axis=-1, keepdims=True)
        states.append((carry, acc))

    def more(state):
        kb, carry, _ = state
        return jnp.logical_and(kb >= 0, jnp.max(carry) > STICK_ZERO_LOG2)

    for cols, (carry, acc) in zip(heads, states):
        def step(state, cols=cols):
            kb, carry, acc = state
            log_beta, log_keep, v = logits(cols, kb)
            a = weights(log_beta, log_keep, carry)
            acc = acc + _dot(a.astype(BF16), v)
            carry = carry + jnp.sum(log_keep, axis=-1, keepdims=True)
            return kb - 1, carry, acc

        _, _, acc = lax.while_loop(more, step, (qi - 2, carry, acc))
        o_ref[:, cols] = acc.astype(BF16)


def _attention(proj, *, bsz, seq, tq, n_heads):
    m = proj.shape[0]
    nq = seq // tq
    width = n_heads * HEAD_DIM
    groups = SEG // width
    return pl.pallas_call(
        functools.partial(_attn_kernel, tq=tq, n_heads=n_heads),
        grid=(bsz, groups, nq),
        in_specs=[
            pl.BlockSpec((tq, width), lambda b, h, i: (b * nq + i, h)),
            pl.BlockSpec((seq, width), lambda b, h, i: (b, groups + h)),
            pl.BlockSpec((seq, width), lambda b, h, i: (b, 2 * groups + h)),
        ],
        out_specs=pl.BlockSpec((tq, width), lambda b, h, i: (b * nq + i, h)),
        out_shape=jax.ShapeDtypeStruct((m, SEG), BF16),
        compiler_params=_params(("parallel", "parallel", "arbitrary")),
        name="sb_attention",
    )(proj, proj, proj)


def _mix_kernel(o_ref, u_ref, vln_ref, ga0_ref, ga1_ref, gb0_ref, gb1_ref, x_ref, ws_ref, bs_ref,
                woa_ref, wob_ref, wout_ref, g_ref, xo_ref, h_ref, s_ref, mg_ref, xs_ref,
                *, tm, tn):
    step = pl.program_id(1)
    d = wout_ref.shape[1]
    n_tiles = d // tn

    @pl.when(step == 0)
    def _():
        ii = lax.broadcasted_iota(jnp.int32, (SGU_LEN, SGU_LEN), 0) // SGU_CHUNK
        jj = lax.broadcasted_iota(jnp.int32, (SGU_LEN, SGU_LEN), 1) // SGU_CHUNK
        allowed = jj <= ii
        for g in range(SEG // HEAD_DIM):
            sl = slice(g * HEAD_DIM, (g + 1) * HEAD_DIM)
            w = jnp.where(allowed, ws_ref[g], 0.0).astype(BF16)
            bias = bs_ref[:, g:g + 1]
            for c in range(tm // SGU_LEN):
                rows = slice(c * SGU_LEN, (c + 1) * SGU_LEN)
                mixed = _dot(w, vln_ref[rows, sl]) + bias
                s_ref[rows, sl] = (u_ref[rows, sl].astype(F32) * mixed).astype(BF16)
        gates_a, gates_b = (ga0_ref, ga1_ref), (gb0_ref, gb1_ref)
        for t in range(n_tiles):
            cols = slice(t * tn, (t + 1) * tn)
            half, sub = divmod(t * tn, SEG)
            gcols = slice(sub, sub + tn)
            y_a = _dot(o_ref[...], woa_ref[:, cols])
            y_b = _dot(s_ref[...], wob_ref[:, cols])
            merged = (gates_a[half][:, gcols].astype(F32) * y_a
                      + gates_b[half][:, gcols].astype(F32) * y_b)
            mg_ref[:, cols] = merged.astype(BF16)

    for t in range(n_tiles):
        @pl.when(step == t + 1)
        def _(t=t):
            cols = slice(t * tn, (t + 1) * tn)
            xn = x_ref[...] + _dot(mg_ref[...], wout_ref[:, cols])
            xo_ref[...] = xn
            xs_ref[t] = xn
            if t == n_tiles - 1:
                ms = sum(jnp.sum(xs_ref[c] * xs_ref[c], axis=-1, keepdims=True)
                         for c in range(n_tiles)) * (1.0 / d)
                inv = lax.rsqrt(ms + EPS)
                for c in range(n_tiles):
                    ccols = slice(c * tn, (c + 1) * tn)
                    h_ref[:, ccols] = (xs_ref[c] * inv * g_ref[:, ccols]).astype(BF16)


def _mix(x, o, proj, ws, bs_t, woa, wob, wout, g, *, layer, tm, tn):
    m, d = x.shape
    assert d == 2 * SEG and SEG % tn == 0
    n_tiles = d // tn
    tile = lambda i, s: (i, jnp.maximum(s - 1, 0))
    once = pl.Buffered(1)
    seg = lambda blk: pl.BlockSpec((tm, SEG), lambda i, s: (i, blk))
    return pl.pallas_call(
        functools.partial(_mix_kernel, tm=tm, tn=tn),
        grid=(m // tm, n_tiles + 1),
        in_specs=[
            seg(0),
            seg(3), seg(4),
            seg(5), seg(6), seg(7), seg(8),
            pl.BlockSpec((tm, tn), tile),
            pl.BlockSpec((None,) + ws.shape[1:], lambda i, s: (layer, 0, 0, 0), pipeline_mode=once),
            pl.BlockSpec((None,) + bs_t.shape[1:], lambda i, s: (layer, 0, 0), pipeline_mode=once),
            pl.BlockSpec((None, SEG, d), lambda i, s: (layer, 0, 0), pipeline_mode=once),
            pl.BlockSpec((None, SEG, d), lambda i, s: (layer, 0, 0), pipeline_mode=once),
            pl.BlockSpec((None, d, d), lambda i, s: (layer, 0, 0), pipeline_mode=once),
            pl.BlockSpec((None, 1, d), lambda i, s: (layer, 0, 0), pipeline_mode=once),
        ],
        out_specs=[
            pl.BlockSpec((tm, tn), tile),
            pl.BlockSpec((tm, d), lambda i, s: (i, 0)),
        ],
        out_shape=[jax.ShapeDtypeStruct((m, d), F32), jax.ShapeDtypeStruct((m, d), BF16)],
        scratch_shapes=[
            pltpu.VMEM((tm, SEG), BF16),
            pltpu.VMEM((tm, d), BF16),
            pltpu.VMEM((n_tiles, tm, tn), F32),
        ],
        compiler_params=_params(("parallel", "arbitrary")),
        name="sgu_merge_out_proj",
    )(o, proj, proj, proj, proj, proj, proj, x, ws, bs_t, woa, wob, wout, g)


def _ffn_kernel(x_ref, h_ref, w1_ref, w2_ref, o_ref, hid_ref, *, n_up):
    s = pl.program_id(1)

    @pl.when(s < n_up)
    def _():
        hid_ref[s] = jnp.square(jnp.maximum(_dot(h_ref[...], w1_ref[...]), 0.0)).astype(BF16)

    @pl.when(s >= n_up)
    def _():
        upd = _dot(hid_ref[0], w2_ref[0])
        for c in range(1, n_up):
            upd = upd + _dot(hid_ref[c], w2_ref[c])
        o_ref[...] = x_ref[...] + upd


def _ffn(x, h, w1, w2, *, layer, tm):
    m, d = x.shape
    _, n_up, _, tf = w1.shape
    _, n_down, _, _, tn = w2.shape
    down = lambda s: jnp.maximum(s - n_up, 0)
    return pl.pallas_call(
        functools.partial(_ffn_kernel, n_up=n_up),
        grid=(m // tm, n_up + n_down),
        in_specs=[
            pl.BlockSpec((tm, tn), lambda i, s: (i, down(s))),
            pl.BlockSpec((tm, d), lambda i, s: (i, 0)),
            pl.BlockSpec((None, None, d, tf), lambda i, s: (layer, jnp.minimum(s, n_up - 1), 0, 0)),
            pl.BlockSpec((None, None, n_up, tf, tn), lambda i, s: (layer, down(s), 0, 0, 0)),
        ],
        out_specs=pl.BlockSpec((tm, tn), lambda i, s: (i, down(s))),
        out_shape=jax.ShapeDtypeStruct((m, d), F32),
        scratch_shapes=[pltpu.VMEM((n_up, tm, tf), BF16)],
        compiler_params=_params(("parallel", "arbitrary")),
        name="ffn",
    )(x, h, w1, w2)


def _tile(n, want):
    t = min(n, want)
    assert n % t == 0
    return t


def _column_tiles(w, width):
    depth, k, n = w.shape
    return w.reshape(depth, k, n // width, width).transpose(0, 2, 1, 3)


def kernel(x, g_mix, w_in, g_q, g_k, sgu_ln_g, sgu_ln_b, w_spatial, b_spatial, w_oa, w_ob, w_out,
           g_ff, w_ff1, w_ff2):
    bsz, seq, d = x.shape
    depth, _, ff = w_ff1.shape
    m = bsz * seq
    assert seq % SGU_LEN == 0 and w_in.shape[2] == 5 * SEG + 2 * d
    xf = x.reshape(m, d)
    w_oa, w_ob, w_out = (w.astype(BF16) for w in (w_oa, w_ob, w_out))
    w_in = _column_tiles(w_in.astype(BF16), SEG)
    w_ff1 = _column_tiles(w_ff1.astype(BF16), FFN_UP_TILE)
    w_ff2 = _column_tiles(w_ff2.astype(BF16), FFN_DOWN_TILE)
    w_ff2 = w_ff2.reshape(depth, d // FFN_DOWN_TILE, ff // FFN_UP_TILE, FFN_UP_TILE, FFN_DOWN_TILE)
    g_mix, g_ff = g_mix.reshape(depth, 1, d), g_ff.reshape(depth, 1, d)
    g_q, g_k, sgu_ln_g, sgu_ln_b = (
        p.reshape(depth, 1, SEG) for p in (g_q, g_k, sgu_ln_g, sgu_ln_b))
    b_spatial_t = b_spatial.transpose(0, 2, 1)
    for l in range(depth):
        proj = _in_proj(xf, g_mix, w_in, g_q, g_k, sgu_ln_g, sgu_ln_b, layer=l, tm=_tile(m, 1024))
        o = _attention(proj, bsz=bsz, seq=seq, tq=_tile(seq, 256), n_heads=2)
        xf, h2 = _mix(xf, o, proj, w_spatial, b_spatial_t, w_oa, w_ob, w_out, g_ff, layer=l,
                      tm=_tile(m, 512), tn=1024)
        xf = _ffn(xf, h2, w_ff1, w_ff2, layer=l, tm=_tile(m, 1024))
    return xf.reshape(bsz, seq, d)
```

```python
import functools

import jax
import jax.numpy as jnp
from jax import lax
from jax.experimental import pallas as pl
from jax.experimental.pallas import tpu as pltpu

EPS = 1e-6
HEAD_DIM = 128
SGU_LEN = 128
SGU_CHUNK = 64
SEG = 1024
MXU_WIDTH = 256
FFN_UP_TILE = 1024
FFN_DOWN_TILE = 256

VMEM_LIMIT_BYTES = 56 * 1024 * 1024

BF16 = jnp.bfloat16
F32 = jnp.float32

LOG2_E = 1.4426950408889634
STICK_ZERO_LOG2 = -160.0


def _params(sem):
    return pltpu.CompilerParams(dimension_semantics=sem, vmem_limit_bytes=VMEM_LIMIT_BYTES)


def _dot(a, b):
    return jnp.dot(a, b, preferred_element_type=F32)


def _gelu(x):
    return 0.5 * x * (1.0 + lax.erf(x * (2.0 ** -0.5)))


def _in_proj_kernel(x_ref, g_ref, w_ref, gq_ref, gk_ref, lng_ref, lnb_ref, o_ref, h_ref, *, scale):
    j = pl.program_id(1)
    chunks = [slice(c, c + MXU_WIDTH) for c in range(0, SEG, MXU_WIDTH)]

    @pl.when(j == 0)
    def _():
        x = x_ref[...]
        ms = jnp.mean(x * x, axis=-1, keepdims=True)
        h_ref[...] = (x * lax.rsqrt(ms + EPS) * g_ref[...]).astype(BF16)

    def chunk_dot(sl):
        return _dot(h_ref[...], w_ref[:, sl])

    def head_norm(gain_ref, mul):
        for sl in chunks:
            acc = chunk_dot(sl)
            for h in range(0, MXU_WIDTH, HEAD_DIM):
                blk = acc[:, h:h + HEAD_DIM]
                cols = slice(sl.start + h, sl.start + h + HEAD_DIM)
                ms = jnp.mean(blk * blk, axis=-1, keepdims=True)
                o_ref[:, cols] = (blk * (lax.rsqrt(ms + EPS) * mul) * gain_ref[:, cols]).astype(BF16)

    @pl.when(j == 0)
    def _():
        head_norm(gq_ref, scale)

    @pl.when(j == 1)
    def _():
        head_norm(gk_ref, 1.0)

    @pl.when(j == 2)
    def _():
        for sl in chunks:
            o_ref[:, sl] = chunk_dot(sl).astype(BF16)

    @pl.when(j == 3)
    def _():
        for sl in chunks:
            o_ref[:, sl] = _gelu(chunk_dot(sl)).astype(BF16)

    @pl.when(j == 4)
    def _():
        vs = [_gelu(chunk_dot(sl)) for sl in chunks]
        mu = sum(jnp.sum(v, axis=-1, keepdims=True) for v in vs) * (1.0 / SEG)
        vs = [v - mu for v in vs]
        var = sum(jnp.sum(v * v, axis=-1, keepdims=True) for v in vs) * (1.0 / SEG)
        inv = lax.rsqrt(var + EPS)
        for sl, v in zip(chunks, vs):
            o_ref[:, sl] = (v * inv * lng_ref[:, sl] + lnb_ref[:, sl]).astype(BF16)

    @pl.when(j >= 5)
    def _():
        for sl in chunks:
            o_ref[:, sl] = jax.nn.sigmoid(chunk_dot(sl)).astype(BF16)


def _in_proj(x, g, w, gq, gk, lng, lnb, *, layer, tm):
    m, d = x.shape
    n = w.shape[2]
    assert n % SEG == 0 and m % tm == 0
    vec = lambda width: pl.BlockSpec((None, 1, width), lambda i, j: (layer, 0, 0))
    return pl.pallas_call(
        functools.partial(_in_proj_kernel, scale=HEAD_DIM ** -0.5 * LOG2_E),
        grid=(m // tm, n // SEG),
        in_specs=[
            pl.BlockSpec((tm, d), lambda i, j: (i, 0)),
            vec(d),
            pl.BlockSpec((None, d, SEG), lambda i, j: (layer, 0, j)),
            vec(SEG), vec(SEG), vec(SEG), vec(SEG),
        ],
        out_specs=pl.BlockSpec((tm, SEG), lambda i, j: (i, j)),
        out_shape=jax.ShapeDtypeStruct((m, n), BF16),
        scratch_shapes=[pltpu.VMEM((tm, d), BF16)],
        compiler_params=_params(("parallel", "arbitrary")),
        name="in_proj",
    )(x, g, w, gq, gk, lng, lnb)


def _attn_kernel(q_ref, k_ref, v_ref, o_ref, *, tq, n_heads):
    tk = tq
    qi = pl.program_id(2)
    jj = lax.broadcasted_iota(jnp.int32, (tk, tk), 0)
    ss = lax.broadcasted_iota(jnp.int32, (tk, tk), 1)
    upper = (jj > ss).astype(BF16)
    upper2 = jnp.concatenate([upper, upper], axis=0)
    past = ss < jj

    def logits(cols, kb):
        k0 = pl.multiple_of(kb * tk, tk)
        z = lax.dot_general(q_ref[:, cols], k_ref[pl.ds(k0, tk), cols], (((1,), (1,)), ((), ())),
                            preferred_element_type=F32)
        soft = jnp.log(1.0 + jnp.exp2(-jnp.abs(z))) * LOG2_E
        log_beta = jnp.minimum(z, 0.0) - soft
        log_keep = log_beta - z
        return log_beta, log_keep, v_ref[pl.ds(k0, tk), cols]

    def weights(log_beta, log_keep, carry):
        hi = log_keep.astype(BF16)
        lo = (log_keep - hi.astype(F32)).astype(BF16)
        tail = _dot(jnp.concatenate([hi, lo], axis=1), upper2)
        return jnp.exp2(log_beta + tail + carry)

    has_prev = (qi > 0).astype(F32)
    heads = [slice(h * HEAD_DIM, (h + 1) * HEAD_DIM) for h in range(n_heads)]
    states = []
    for cols in heads:
        log_beta, log_keep, v = logits(cols, qi)
        log_keep = jnp.where(past, log_keep, 0.0)
        a = jnp.where(past, weights(log_beta, log_keep, 0.0), 0.0)
        acc = _dot(a.astype(BF16), v)
        carry = jnp.sum(log_keep, axis=-1, keepdims=True)
        log_beta, log_keep, v = logits(cols, jnp.maximum(qi - 1, 0))
        a = weights(log_beta, log_keep, carry)
        acc = acc + has_prev * _dot(a.astype(BF16), v)
        carry = carry + jnp.sum(log_keep, axis=-1, keepdims=True)
        states.append((carry, acc))

    def more(state):
        kb, carry, _ = state
        return jnp.logical_and(kb >= 0, jnp.max(carry) > STICK_ZERO_LOG2)

    for cols, (carry, acc) in zip(heads, states):
        def step(state, cols=cols):
            kb, carry, acc = state
            log_beta, log_keep, v = logits(cols, kb)
            a = weights(log_beta, log_keep, carry)
            acc = acc + _dot(a.astype(BF16), v)
            carry = carry + jnp.sum(log_keep, axis=-1, keepdims=True)
            return kb - 1, carry, acc

        _, _, acc = lax.while_loop(more, step, (qi - 2, carry, acc))
        o_ref[:, cols] = acc.astype(BF16)


def _attention(proj, *, bsz, seq, tq, n_heads):
    m = proj.shape[0]
    nq = seq // tq
    width = n_heads * HEAD_DIM
    groups = SEG // width
    return pl.pallas_call(
        functools.partial(_attn_kernel, tq=tq, n_heads=n_heads),
        grid=(bsz, groups, nq),
        in_specs=[
            pl.BlockSpec((tq, width), lambda b, h, i: (b * nq + i, h)),
            pl.BlockSpec((seq, width), lambda b, h, i: (b, groups + h)),
            pl.BlockSpec((seq, width), lambda b, h, i: (b, 2 * groups + h)),
        ],
        out_specs=pl.BlockSpec((tq, width), lambda b, h, i: (b * nq + i, h)),
        out_shape=jax.ShapeDtypeStruct((m, SEG), BF16),
        compiler_params=_params(("parallel", "parallel", "arbitrary")),
        name="sb_attention",
    )(proj, proj, proj)


def _merge_kernel(o_ref, u_ref, vln_ref, ws_ref, bs_ref, woa_ref, wob_ref, ga_ref, gb_ref,
                  out_ref, s_ref, *, tm):
    j = pl.program_id(1)

    def spatial_gating():
        ii = lax.broadcasted_iota(jnp.int32, (SGU_LEN, SGU_LEN), 0) // SGU_CHUNK
        jj = lax.broadcasted_iota(jnp.int32, (SGU_LEN, SGU_LEN), 1) // SGU_CHUNK
        allowed = jj <= ii
        for g in range(SEG // HEAD_DIM):
            sl = slice(g * HEAD_DIM, (g + 1) * HEAD_DIM)
            w = jnp.where(allowed, ws_ref[g], 0.0).astype(BF16)
            bias = bs_ref[:, g:g + 1]
            for c in range(tm // SGU_LEN):
                rows = slice(c * SGU_LEN, (c + 1) * SGU_LEN)
                mixed = _dot(w, vln_ref[rows, sl]) + bias
                s_ref[rows, sl] = (u_ref[rows, sl].astype(F32) * mixed).astype(BF16)

    def project():
        y_a = _dot(o_ref[...], woa_ref[...])
        y_b = _dot(s_ref[...], wob_ref[...])
        merged = ga_ref[...].astype(F32) * y_a + gb_ref[...].astype(F32) * y_b
        out_ref[...] = merged.astype(BF16)

    @pl.when(j == 0)
    def _():
        spatial_gating()
        project()

    @pl.when(j > 0)
    def _():
        project()


def _merge(o, proj, ws, bs_t, woa, wob, *, layer, tm, tn):
    m = o.shape[0]
    d = woa.shape[2]
    u_blk = 3
    v_blk = 4
    ga_blk = 5 * SEG // tn
    gb_blk = ga_blk + d // tn
    return pl.pallas_call(
        functools.partial(_merge_kernel, tm=tm),
        grid=(m // tm, d // tn),
        in_specs=[
            pl.BlockSpec((tm, SEG), lambda i, j: (i, 0)),
            pl.BlockSpec((tm, SEG), lambda i, j: (i, u_blk)),
            pl.BlockSpec((tm, SEG), lambda i, j: (i, v_blk)),
            pl.BlockSpec((None,) + ws.shape[1:], lambda i, j: (layer, 0, 0, 0)),
            pl.BlockSpec((None,) + bs_t.shape[1:], lambda i, j: (layer, 0, 0)),
            pl.BlockSpec((None, SEG, tn), lambda i, j: (layer, 0, j)),
            pl.BlockSpec((None, SEG, tn), lambda i, j: (layer, 0, j)),
            pl.BlockSpec((tm, tn), lambda i, j: (i, ga_blk + j)),
            pl.BlockSpec((tm, tn), lambda i, j: (i, gb_blk + j)),
        ],
        out_specs=pl.BlockSpec((tm, tn), lambda i, j: (i, j)),
        out_shape=jax.ShapeDtypeStruct((m, d), BF16),
        scratch_shapes=[pltpu.VMEM((tm, SEG), BF16)],
        compiler_params=_params(("parallel", "arbitrary")),
        name="sgu_merge",
    )(o, proj, proj, ws, bs_t, woa, wob, proj, proj)


def _out_proj_kernel(x_ref, mg_ref, w_ref, g_ref, xo_ref, h_ref):
    x = x_ref[...] + _dot(mg_ref[...], w_ref[...])
    xo_ref[...] = x
    ms = jnp.mean(x * x, axis=-1, keepdims=True)
    h_ref[...] = (x * lax.rsqrt(ms + EPS) * g_ref[...]).astype(BF16)


def _out_proj(x, merged, w, g, *, layer, tm):
    m, d = x.shape
    return pl.pallas_call(
        _out_proj_kernel,
        grid=(m // tm,),
        in_specs=[
            pl.BlockSpec((tm, d), lambda i: (i, 0)),
            pl.BlockSpec((tm, d), lambda i: (i, 0)),
            pl.BlockSpec((None, d, d), lambda i: (layer, 0, 0)),
            pl.BlockSpec((None, 1, d), lambda i: (layer, 0, 0)),
        ],
        out_specs=[
            pl.BlockSpec((tm, d), lambda i: (i, 0)),
            pl.BlockSpec((tm, d), lambda i: (i, 0)),
        ],
        out_shape=[jax.ShapeDtypeStruct((m, d), F32), jax.ShapeDtypeStruct((m, d), BF16)],
        compiler_params=_params(("parallel",)),
        name="out_proj",
    )(x, merged, w, g)


def _ffn_kernel(x_ref, h_ref, w1_ref, w2_ref, o_ref, hid_ref, *, n_up):
    s = pl.program_id(1)

    @pl.when(s < n_up)
    def _():
        hid_ref[s] = jnp.square(jnp.maximum(_dot(h_ref[...], w1_ref[...]), 0.0)).astype(BF16)

    @pl.when(s >= n_up)
    def _():
        upd = _dot(hid_ref[0], w2_ref[0])
        for c in range(1, n_up):
            upd = upd + _dot(hid_ref[c], w2_ref[c])
        o_ref[...] = x_ref[...] + upd


def _ffn(x, h, w1, w2, *, layer, tm):
    m, d = x.shape
    _, n_down, n_up, tf, tn = w2.shape
    down = lambda s: jnp.maximum(s - n_up, 0)
    return pl.pallas_call(
        functools.partial(_ffn_kernel, n_up=n_up),
        grid=(m // tm, n_up + n_down),
        in_specs=[
            pl.BlockSpec((tm, tn), lambda i, s: (i, down(s))),
            pl.BlockSpec((tm, d), lambda i, s: (i, 0)),
            pl.BlockSpec((None, d, tf), lambda i, s: (layer, 0, jnp.minimum(s, n_up - 1))),
            pl.BlockSpec((None, None, n_up, tf, tn), lambda i, s: (layer, down(s), 0, 0, 0)),
        ],
        out_specs=pl.BlockSpec((tm, tn), lambda i, s: (i, down(s))),
        out_shape=jax.ShapeDtypeStruct((m, d), F32),
        scratch_shapes=[pltpu.VMEM((n_up, tm, tf), BF16)],
        compiler_params=_params(("parallel", "arbitrary")),
        name="ffn",
    )(x, h, w1, w2)


def _tile_w2_kernel(w_ref, o_ref):
    tn = o_ref.shape[-1]
    for t in range(o_ref.shape[0]):
        o_ref[t] = w_ref[:, t * tn:(t + 1) * tn].astype(BF16)


def _tile_w2(w2, *, tf, tn):
    depth, ff, d = w2.shape
    return pl.pallas_call(
        _tile_w2_kernel,
        grid=(depth, ff // tf),
        in_specs=[pl.BlockSpec((None, tf, d), lambda l, c: (l, c, 0))],
        out_specs=pl.BlockSpec((None, d // tn, None, tf, tn), lambda l, c: (l, 0, c, 0, 0)),
        out_shape=jax.ShapeDtypeStruct((depth, d // tn, ff // tf, tf, tn), BF16),
        compiler_params=_params(("parallel", "parallel")),
        name="tile_w2",
    )(w2)


def _tile(n, want):
    t = min(n, want)
    assert n % t == 0
    return t


def kernel(x, g_mix, w_in, g_q, g_k, sgu_ln_g, sgu_ln_b, w_spatial, b_spatial, w_oa, w_ob, w_out,
           g_ff, w_ff1, w_ff2):
    bsz, seq, d = x.shape
    depth = w_in.shape[0]
    m = bsz * seq
    assert seq % SGU_LEN == 0 and w_in.shape[2] == 5 * SEG + 2 * d
    xf = x.reshape(m, d)
    w_in, w_oa, w_ob, w_out, w_ff1 = (w.astype(BF16) for w in (w_in, w_oa, w_ob, w_out, w_ff1))
    w_ff2 = _tile_w2(w_ff2, tf=FFN_UP_TILE, tn=FFN_DOWN_TILE)
    g_mix, g_ff = g_mix.reshape(depth, 1, d), g_ff.reshape(depth, 1, d)
    g_q, g_k, sgu_ln_g, sgu_ln_b = (
        p.reshape(depth, 1, SEG) for p in (g_q, g_k, sgu_ln_g, sgu_ln_b))
    b_spatial_t = b_spatial.transpose(0, 2, 1)
    for l in range(depth):
        proj = _in_proj(xf, g_mix, w_in, g_q, g_k, sgu_ln_g, sgu_ln_b, layer=l, tm=_tile(m, 1024))
        o = _attention(proj, bsz=bsz, seq=seq, tq=_tile(seq, 256), n_heads=4)
        merged = _merge(o, proj, w_spatial, b_spatial_t, w_oa, w_ob, layer=l,
                        tm=_tile(m, 1024), tn=512)
        xf, h2 = _out_proj(xf, merged, w_out, g_ff, layer=l, tm=_tile(m, 512))
        xf = _ffn(xf, h2, w_ff1, w_ff2, layer=l, tm=_tile(m, 1024))
    return xf.reshape(bsz, seq, d)
```

```python
import functools

import jax
import jax.numpy as jnp
from jax import lax
from jax.experimental import pallas as pl
from jax.experimental.pallas import tpu as pltpu

EPS = 1e-6
HEAD_DIM = 128
SGU_LEN = 128
SGU_CHUNK = 64
SEG = 1024
MXU_WIDTH = 256
FFN_UP_TILE = 1024
FFN_DOWN_TILE = 256

VMEM_LIMIT_BYTES = 56 * 1024 * 1024

BF16 = jnp.bfloat16
F32 = jnp.float32

LOG2_E = 1.4426950408889634
STICK_ZERO_LOG2 = -160.0


def _params(sem):
    return pltpu.CompilerParams(dimension_semantics=sem, vmem_limit_bytes=VMEM_LIMIT_BYTES)


def _dot(a, b):
    return jnp.dot(a, b, preferred_element_type=F32)


def _gelu(x):
    return 0.5 * x * (1.0 + lax.erf(x * (2.0 ** -0.5)))


def _in_proj_kernel(x_ref, g_ref, w_ref, gq_ref, gk_ref, lng_ref, lnb_ref, o_ref, h_ref, *, scale):
    j = pl.program_id(1)
    chunks = [slice(c, c + MXU_WIDTH) for c in range(0, SEG, MXU_WIDTH)]

    @pl.when(j == 0)
    def _():
        x = x_ref[...]
        ms = jnp.mean(x * x, axis=-1, keepdims=True)
        h_ref[...] = (x * lax.rsqrt(ms + EPS) * g_ref[...]).astype(BF16)

    def chunk_dot(sl):
        return _dot(h_ref[...], w_ref[:, sl])

    def head_norm(gain_ref, mul):
        for sl in chunks:
            acc = chunk_dot(sl)
            for h in range(0, MXU_WIDTH, HEAD_DIM):
                blk = acc[:, h:h + HEAD_DIM]
                cols = slice(sl.start + h, sl.start + h + HEAD_DIM)
                ms = jnp.mean(blk * blk, axis=-1, keepdims=True)
                o_ref[:, cols] = (blk * (lax.rsqrt(ms + EPS) * mul) * gain_ref[:, cols]).astype(BF16)

    @pl.when(j == 0)
    def _():
        head_norm(gq_ref, scale)

    @pl.when(j == 1)
    def _():
        head_norm(gk_ref, 1.0)

    @pl.when(j == 2)
    def _():
        for sl in chunks:
            o_ref[:, sl] = chunk_dot(sl).astype(BF16)

    @pl.when(j == 3)
    def _():
        for sl in chunks:
            o_ref[:, sl] = _gelu(chunk_dot(sl)).astype(BF16)

    @pl.when(j == 4)
    def _():
        vs = [_gelu(chunk_dot(sl)) for sl in chunks]
        mu = sum(jnp.sum(v, axis=-1, keepdims=True) for v in vs) * (1.0 / SEG)
        vs = [v - mu for v in vs]
        var = sum(jnp.sum(v * v, axis=-1, keepdims=True) for v in vs) * (1.0 / SEG)
        inv = lax.rsqrt(var + EPS)
        for sl, v in zip(chunks, vs):
            o_ref[:, sl] = (v * inv * lng_ref[:, sl] + lnb_ref[:, sl]).astype(BF16)

    @pl.when(j >= 5)
    def _():
        for sl in chunks:
            o_ref[:, sl] = jax.nn.sigmoid(chunk_dot(sl)).astype(BF16)


def _in_proj(x, g, w, gq, gk, lng, lnb, *, layer, tm):
    m, d = x.shape
    n = w.shape[2]
    assert n % SEG == 0 and m % tm == 0
    vec = lambda width: pl.BlockSpec((None, 1, width), lambda i, j: (layer, 0, 0))
    return pl.pallas_call(
        functools.partial(_in_proj_kernel, scale=HEAD_DIM ** -0.5 * LOG2_E),
        grid=(m // tm, n // SEG),
        in_specs=[
            pl.BlockSpec((tm, d), lambda i, j: (i, 0)),
            vec(d),
            pl.BlockSpec((None, d, SEG), lambda i, j: (layer, 0, j)),
            vec(SEG), vec(SEG), vec(SEG), vec(SEG),
        ],
        out_specs=pl.BlockSpec((tm, SEG), lambda i, j: (i, j)),
        out_shape=jax.ShapeDtypeStruct((m, n), BF16),
        scratch_shapes=[pltpu.VMEM((tm, d), BF16)],
        compiler_params=_params(("parallel", "arbitrary")),
        name="in_proj",
    )(x, g, w, gq, gk, lng, lnb)


def _attn_kernel(q_ref, k_ref, v_ref, o_ref, *, tq, n_heads):
    tk = tq
    qi = pl.program_id(2)
    jj = lax.broadcasted_iota(jnp.int32, (tk, tk), 0)
    ss = lax.broadcasted_iota(jnp.int32, (tk, tk), 1)
    upper = (jj > ss).astype(BF16)
    past = ss < jj

    def logits(cols, kb):
        k0 = pl.multiple_of(kb * tk, tk)
        z = lax.dot_general(q_ref[:, cols], k_ref[pl.ds(k0, tk), cols], (((1,), (1,)), ((), ())),
                            preferred_element_type=F32)
        soft = jnp.log(1.0 + jnp.exp2(-jnp.abs(z))) * LOG2_E
        log_beta = jnp.minimum(z, 0.0) - soft
        log_keep = log_beta - z
        return log_beta, log_keep, v_ref[pl.ds(k0, tk), cols]

    def weights(log_beta, log_keep, carry):
        tail = _dot(log_keep.astype(BF16), upper)
        return jnp.exp2(log_beta + tail + carry)

    has_prev = (qi > 0).astype(F32)
    heads = [slice(h * HEAD_DIM, (h + 1) * HEAD_DIM) for h in range(n_heads)]
    states = []
    for cols in heads:
        log_beta, log_keep, v = logits(cols, qi)
        log_keep = jnp.where(past, log_keep, 0.0)
        a = jnp.where(past, weights(log_beta, log_keep, 0.0), 0.0)
        acc = _dot(a.astype(BF16), v)
        carry = jnp.sum(log_keep, axis=-1, keepdims=True)
        log_beta, log_keep, v = logits(cols, jnp.maximum(qi - 1, 0))
        a = weights(log_beta, log_keep, carry)
        acc = acc + has_prev * _dot(a.astype(BF16), v)
        carry = carry + jnp.sum(log_keep, axis=-1, keepdims=True)
        states.append((carry, acc))

    def more(state):
        kb, carry, _ = state
        return jnp.logical_and(kb >= 0, jnp.max(carry) > STICK_ZERO_LOG2)

    for cols, (carry, acc) in zip(heads, states):
        def step(state, cols=cols):
            kb, carry, acc = state
            log_beta, log_keep, v = logits(cols, kb)
            a = weights(log_beta, log_keep, carry)
            acc = acc + _dot(a.astype(BF16), v)
            carry = carry + jnp.sum(log_keep, axis=-1, keepdims=True)
            return kb - 1, carry, acc

        _, _, acc = lax.while_loop(more, step, (qi - 2, carry, acc))
        o_ref[:, cols] = acc.astype(BF16)


def _attention(proj, *, bsz, seq, tq, n_heads):
    m = proj.shape[0]
    nq = seq // tq
    width = n_heads * HEAD_DIM
    groups = SEG // width
    return pl.pallas_call(
        functools.partial(_attn_kernel, tq=tq, n_heads=n_heads),
        grid=(bsz, groups, nq),
        in_specs=[
            pl.BlockSpec((tq, width), lambda b, h, i: (b * nq + i, h)),
            pl.BlockSpec((seq, width), lambda b, h, i: (b, groups + h)),
            pl.BlockSpec((seq, width), lambda b, h, i: (b, 2 * groups + h)),
        ],
        out_specs=pl.BlockSpec((tq, width), lambda b, h, i: (b * nq + i, h)),
        out_shape=jax.ShapeDtypeStruct((m, SEG), BF16),
        compiler_params=_params(("parallel", "parallel", "arbitrary")),
        name="sb_attention",
    )(proj, proj, proj)


def _merge_kernel(o_ref, u_ref, vln_ref, ws_ref, bs_ref, woa_ref, wob_ref, ga_ref, gb_ref,
                  out_ref, s_ref, *, tm):
    j = pl.program_id(1)

    def spatial_gating():
        ii = lax.broadcasted_iota(jnp.int32, (SGU_LEN, SGU_LEN), 0) // SGU_CHUNK
        jj = lax.broadcasted_iota(jnp.int32, (SGU_LEN, SGU_LEN), 1) // SGU_CHUNK
        allowed = jj <= ii
        for g in range(SEG // HEAD_DIM):
            sl = slice(g * HEAD_DIM, (g + 1) * HEAD_DIM)
            w = jnp.where(allowed, ws_ref[g], 0.0).astype(BF16)
            bias = bs_ref[:, g:g + 1]
            for c in range(tm // SGU_LEN):
                rows = slice(c * SGU_LEN, (c + 1) * SGU_LEN)
                mixed = _dot(w, vln_ref[rows, sl]) + bias
                s_ref[rows, sl] = (u_ref[rows, sl].astype(F32) * mixed).astype(BF16)

    def project():
        y_a = _dot(o_ref[...], woa_ref[...])
        y_b = _dot(s_ref[...], wob_ref[...])
        merged = ga_ref[...].astype(F32) * y_a + gb_ref[...].astype(F32) * y_b
        out_ref[...] = merged.astype(BF16)

    @pl.when(j == 0)
    def _():
        spatial_gating()
        project()

    @pl.when(j > 0)
    def _():
        project()


def _merge(o, proj, ws, bs_t, woa, wob, *, layer, tm, tn):
    m = o.shape[0]
    d = woa.shape[2]
    u_blk = 3
    v_blk = 4
    ga_blk = 5 * SEG // tn
    gb_blk = ga_blk + d // tn
    last = m // tm - 1

    def ahead(blk, from_step):
        return lambda i, j: (jnp.minimum(i + (j >= from_step).astype(jnp.int32), last), blk)

    return pl.pallas_call(
        functools.partial(_merge_kernel, tm=tm),
        grid=(m // tm, d // tn),
        in_specs=[
            pl.BlockSpec((tm, SEG), lambda i, j: (i, 0)),
            pl.BlockSpec((tm, SEG), ahead(u_blk, 1)),
            pl.BlockSpec((tm, SEG), ahead(v_blk, 2)),
            pl.BlockSpec((None,) + ws.shape[1:], lambda i, j: (layer, 0, 0, 0)),
            pl.BlockSpec((None,) + bs_t.shape[1:], lambda i, j: (layer, 0, 0)),
            pl.BlockSpec((None, SEG, tn), lambda i, j: (layer, 0, j)),
            pl.BlockSpec((None, SEG, tn), lambda i, j: (layer, 0, j)),
            pl.BlockSpec((tm, tn), lambda i, j: (i, ga_blk + j)),
            pl.BlockSpec((tm, tn), lambda i, j: (i, gb_blk + j)),
        ],
        out_specs=pl.BlockSpec((tm, tn), lambda i, j: (i, j)),
        out_shape=jax.ShapeDtypeStruct((m, d), BF16),
        scratch_shapes=[pltpu.VMEM((tm, SEG), BF16)],
        compiler_params=_params(("parallel", "arbitrary")),
        name="sgu_merge",
    )(o, proj, proj, ws, bs_t, woa, wob, proj, proj)


def _out_proj_kernel(x_ref, mg_ref, w_ref, g_ref, xo_ref, h_ref):
    x = x_ref[...] + _dot(mg_ref[...], w_ref[...])
    xo_ref[...] = x
    ms = jnp.mean(x * x, axis=-1, keepdims=True)
    h_ref[...] = (x * lax.rsqrt(ms + EPS) * g_ref[...]).astype(BF16)


def _out_proj(x, merged, w, g, *, layer, tm):
    m, d = x.shape
    return pl.pallas_call(
        _out_proj_kernel,
        grid=(m // tm,),
        in_specs=[
            pl.BlockSpec((tm, d), lambda i: (i, 0)),
            pl.BlockSpec((tm, d), lambda i: (i, 0)),
            pl.BlockSpec((None, d, d), lambda i: (layer, 0, 0)),
            pl.BlockSpec((None, 1, d), lambda i: (layer, 0, 0)),
        ],
        out_specs=[
            pl.BlockSpec((tm, d), lambda i: (i, 0)),
            pl.BlockSpec((tm, d), lambda i: (i, 0)),
        ],
        out_shape=[jax.ShapeDtypeStruct((m, d), F32), jax.ShapeDtypeStruct((m, d), BF16)],
        compiler_params=_params(("parallel",)),
        name="out_proj",
    )(x, merged, w, g)


def _ffn_kernel(x_ref, h_ref, w1_ref, w2_ref, o_ref, hid_ref, *, n_up):
    s = pl.program_id(1)

    @pl.when(s < n_up)
    def _():
        hid_ref[s] = jnp.square(jnp.maximum(_dot(h_ref[...], w1_ref[...]), 0.0)).astype(BF16)

    @pl.when(s >= n_up)
    def _():
        upd = _dot(hid_ref[0], w2_ref[0])
        for c in range(1, n_up):
            upd = upd + _dot(hid_ref[c], w2_ref[c])
        o_ref[...] = x_ref[...] + upd


def _ffn(x, h, w1, w2, *, layer, tm):
    m, d = x.shape
    _, n_down, n_up, tf, tn = w2.shape
    down = lambda s: jnp.maximum(s - n_up, 0)
    return pl.pallas_call(
        functools.partial(_ffn_kernel, n_up=n_up),
        grid=(m // tm, n_up + n_down),
        in_specs=[
            pl.BlockSpec((tm, tn), lambda i, s: (i, down(s))),
            pl.BlockSpec((tm, d), lambda i, s: (i, 0)),
            pl.BlockSpec((None, d, tf), lambda i, s: (layer, 0, jnp.minimum(s, n_up - 1))),
            pl.BlockSpec((None, None, n_up, tf, tn), lambda i, s: (layer, down(s), 0, 0, 0)),
        ],
        out_specs=pl.BlockSpec((tm, tn), lambda i, s: (i, down(s))),
        out_shape=jax.ShapeDtypeStruct((m, d), F32),
        scratch_shapes=[pltpu.VMEM((n_up, tm, tf), BF16)],
        compiler_params=_params(("parallel", "arbitrary")),
        name="ffn",
    )(x, h, w1, w2)


def _tile_w2_kernel(w_ref, o_ref):
    tn = o_ref.shape[-1]
    for t in range(o_ref.shape[0]):
        o_ref[t] = w_ref[:, t * tn:(t + 1) * tn].astype(BF16)


def _tile_w2(w2, *, tf, tn):
    depth, ff, d = w2.shape
    return pl.pallas_call(
        _tile_w2_kernel,
        grid=(depth, ff // tf),
        in_specs=[pl.BlockSpec((None, tf, d), lambda l, c: (l, c, 0))],
        out_specs=pl.BlockSpec((None, d // tn, None, tf, tn), lambda l, c: (l, 0, c, 0, 0)),
        out_shape=jax.ShapeDtypeStruct((depth, d // tn, ff // tf, tf, tn), BF16),
        compiler_params=_params(("parallel", "parallel")),
        name="tile_w2",
    )(w2)


def _tile(n, want):
    t = min(n, want)
    assert n % t == 0
    return t


def kernel(x, g_mix, w_in, g_q, g_k, sgu_ln_g, sgu_ln_b, w_spatial, b_spatial, w_oa, w_ob, w_out,
           g_ff, w_ff1, w_ff2):
    bsz, seq, d = x.shape
    depth = w_in.shape[0]
    m = bsz * seq
    assert seq % SGU_LEN == 0 and w_in.shape[2] == 5 * SEG + 2 * d
    xf = x.reshape(m, d)
    w_in, w_oa, w_ob, w_out, w_ff1 = (w.astype(BF16) for w in (w_in, w_oa, w_ob, w_out, w_ff1))
    w_ff2 = _tile_w2(w_ff2, tf=FFN_UP_TILE, tn=FFN_DOWN_TILE)
    g_mix, g_ff = g_mix.reshape(depth, 1, d), g_ff.reshape(depth, 1, d)
    g_q, g_k, sgu_ln_g, sgu_ln_b = (
        p.reshape(depth, 1, SEG) for p in (g_q, g_k, sgu_ln_g, sgu_ln_b))
    b_spatial_t = b_spatial.transpose(0, 2, 1)
    for l in range(depth):
        proj = _in_proj(xf, g_mix, w_in, g_q, g_k, sgu_ln_g, sgu_ln_b, layer=l, tm=_tile(m, 1024))
        o = _attention(proj, bsz=bsz, seq=seq, tq=_tile(seq, 256), n_heads=4)
        merged = _merge(o, proj, w_spatial, b_spatial_t, w_oa, w_ob, layer=l,
                        tm=_tile(m, 2048), tn=256)
        xf, h2 = _out_proj(xf, merged, w_out, g_ff, layer=l, tm=_tile(m, 512))
        xf = _ffn(xf, h2, w_ff1, w_ff2, layer=l, tm=_tile(m, 1024))
    return xf.reshape(bsz, seq, d)
```

```python
import functools

import jax
import jax.numpy as jnp
from jax import lax
from jax.experimental import pallas as pl
from jax.experimental.pallas import tpu as pltpu

EPS = 1e-6
HEAD_DIM = 128
SGU_LEN = 128
SGU_CHUNK = 64
SEG = 1024
MXU_WIDTH = 256
FFN_UP_TILE = 1024
FFN_DOWN_TILE = 256

VMEM_LIMIT_BYTES = 56 * 1024 * 1024

BF16 = jnp.bfloat16
F32 = jnp.float32

LOG2_E = 1.4426950408889634
STICK_ZERO_LOG2 = -160.0


def _params(sem):
    return pltpu.CompilerParams(dimension_semantics=sem, vmem_limit_bytes=VMEM_LIMIT_BYTES)


def _dot(a, b):
    return jnp.dot(a, b, preferred_element_type=F32)


def _gelu(x):
    return 0.5 * x * (1.0 + lax.erf(x * (2.0 ** -0.5)))


def _in_proj_kernel(x_ref, g_ref, w_ref, gq_ref, gk_ref, lng_ref, lnb_ref, o_ref, h_ref, *, scale):
    j = pl.program_id(1)
    chunks = [slice(c, c + MXU_WIDTH) for c in range(0, SEG, MXU_WIDTH)]

    @pl.when(j == 0)
    def _():
        x = x_ref[...]
        ms = jnp.mean(x * x, axis=-1, keepdims=True)
        h_ref[...] = (x * lax.rsqrt(ms + EPS) * g_ref[...]).astype(BF16)

    def chunk_dot(sl):
        return _dot(h_ref[...], w_ref[:, sl])

    def head_norm(gain_ref, mul):
        for sl in chunks:
            acc = chunk_dot(sl)
            for h in range(0, MXU_WIDTH, HEAD_DIM):
                blk = acc[:, h:h + HEAD_DIM]
                cols = slice(sl.start + h, sl.start + h + HEAD_DIM)
                ms = jnp.mean(blk * blk, axis=-1, keepdims=True)
                o_ref[:, cols] = (blk * (lax.rsqrt(ms + EPS) * mul) * gain_ref[:, cols]).astype(BF16)

    @pl.when(j == 0)
    def _():
        head_norm(gq_ref, scale)

    @pl.when(j == 1)
    def _():
        head_norm(gk_ref, 1.0)

    @pl.when(j == 2)
    def _():
        for sl in chunks:
            o_ref[:, sl] = chunk_dot(sl).astype(BF16)

    @pl.when(j == 3)
    def _():
        for sl in chunks:
            o_ref[:, sl] = _gelu(chunk_dot(sl)).astype(BF16)

    @pl.when(j == 4)
    def _():
        vs = [_gelu(chunk_dot(sl)) for sl in chunks]
        mu = sum(jnp.sum(v, axis=-1, keepdims=True) for v in vs) * (1.0 / SEG)
        vs = [v - mu for v in vs]
        var = sum(jnp.sum(v * v, axis=-1, keepdims=True) for v in vs) * (1.0 / SEG)
        inv = lax.rsqrt(var + EPS)
        for sl, v in zip(chunks, vs):
            o_ref[:, sl] = (v * inv * lng_ref[:, sl] + lnb_ref[:, sl]).astype(BF16)

    @pl.when(j >= 5)
    def _():
        for sl in chunks:
            o_ref[:, sl] = jax.nn.sigmoid(chunk_dot(sl)).astype(BF16)


def _in_proj(x, g, w, gq, gk, lng, lnb, *, layer, tm):
    m, d = x.shape
    n = w.shape[2]
    assert n % SEG == 0 and m % tm == 0
    vec = lambda width: pl.BlockSpec((None, 1, width), lambda i, j: (layer, 0, 0))
    return pl.pallas_call(
        functools.partial(_in_proj_kernel, scale=HEAD_DIM ** -0.5 * LOG2_E),
        grid=(m // tm, n // SEG),
        in_specs=[
            pl.BlockSpec((tm, d), lambda i, j: (i, 0)),
            vec(d),
            pl.BlockSpec((None, d, SEG), lambda i, j: (layer, 0, j)),
            vec(SEG), vec(SEG), vec(SEG), vec(SEG),
        ],
        out_specs=pl.BlockSpec((tm, SEG), lambda i, j: (i, j)),
        out_shape=jax.ShapeDtypeStruct((m, n), BF16),
        scratch_shapes=[pltpu.VMEM((tm, d), BF16)],
        compiler_params=_params(("parallel", "arbitrary")),
        name="in_proj",
    )(x, g, w, gq, gk, lng, lnb)


def _attn_kernel(q_ref, k_ref, v_ref, o_ref, *, tq, n_heads):
    half = tq // 2
    qi = pl.program_id(2)
    heads = [slice(h * HEAD_DIM, (h + 1) * HEAD_DIM) for h in range(n_heads)]

    def strictly_upper(n):
        return (lax.broadcasted_iota(jnp.int32, (n, n), 0)
                > lax.broadcasted_iota(jnp.int32, (n, n), 1)).astype(BF16)

    def log_sigmoids(z):
        soft = jnp.log(1.0 + jnp.exp2(-jnp.abs(z))) * LOG2_E
        log_beta = jnp.minimum(z, 0.0) - soft
        return log_beta, log_beta - z

    def qk(q, k):
        return lax.dot_general(q, k, (((1,), (1,)), ((), ())), preferred_element_type=F32)

    def row_sum(x):
        return jnp.sum(x, axis=-1, keepdims=True)

    @pl.when(qi == 0)
    def _():
        rows = lax.broadcasted_iota(jnp.int32, (tq, tq), 0)
        keys = lax.broadcasted_iota(jnp.int32, (tq, tq), 1)
        past = keys < rows
        upper = strictly_upper(tq)
        for cols in heads:
            log_beta, log_keep = log_sigmoids(qk(q_ref[:, cols], k_ref[pl.ds(0, tq), cols]))
            log_keep = jnp.where(past, log_keep, 0.0)
            a = jnp.where(past, jnp.exp2(log_beta + _dot(log_keep.astype(BF16), upper)), 0.0)
            o_ref[:, cols] = _dot(a.astype(BF16), v_ref[pl.ds(0, tq), cols]).astype(BF16)

    @pl.when(qi > 0)
    def _():
        win = 3 * half
        w0 = pl.multiple_of((qi - 1) * tq, tq)
        upper_win = strictly_upper(win)
        upper_blk = strictly_upper(tq)
        r = lax.broadcasted_iota(jnp.int32, (tq, half), 0)
        c = lax.broadcasted_iota(jnp.int32, (tq, half), 1)
        past = c < jnp.where(r >= half, r - half, r)
        key_in_blk = lax.broadcasted_iota(jnp.int32, (half, tq), 1)
        zero = jnp.zeros((half, half), BF16)

        def stacked_logits(cols):
            z = qk(q_ref[:, cols], k_ref[pl.ds(w0, 2 * tq), cols])
            return jnp.concatenate([z[:half, :win], z[half:, half:]], axis=0)

        def masked_keep(log_keep):
            return jnp.concatenate(
                [log_keep[:, :tq], jnp.where(past, log_keep[:, tq:], 0.0)], axis=1)

        def unstack(a):
            a = jnp.concatenate([a[:, :tq], jnp.where(past, a[:, tq:], 0.0)], axis=1).astype(BF16)
            return jnp.concatenate([jnp.concatenate([a[:half], zero], axis=1),
                                    jnp.concatenate([zero, a[half:]], axis=1)], axis=0)

        zs = [stacked_logits(cols) for cols in heads]
        logs = [log_sigmoids(z) for z in zs]
        keeps = [masked_keep(log_keep) for _, log_keep in logs]
        tails = [_dot(log_keep.astype(BF16), upper_win) for log_keep in keeps]
        weights = [unstack(jnp.exp2(log_beta + tail)) for (log_beta, _), tail in zip(logs, tails)]
        accs = [_dot(a, v_ref[pl.ds(w0, 2 * tq), cols]) for a, cols in zip(weights, heads)]
        states = [(row_sum(log_keep), acc) for log_keep, acc in zip(keeps, accs)]

        def more(state):
            end, carry, _ = state
            return jnp.logical_and(end > 0, jnp.max(carry) > STICK_ZERO_LOG2)

        for cols, (carry, acc) in zip(heads, states):
            for h in range(2):
                rows = slice(h * half, (h + 1) * half)

                def step(state, cols=cols, rows=rows):
                    end, carry, acc = state
                    k0 = pl.multiple_of(jnp.maximum(end - tq, 0), half)
                    fresh = key_in_blk + k0 < end
                    log_beta, log_keep = log_sigmoids(
                        qk(q_ref[rows, cols], k_ref[pl.ds(k0, tq), cols]))
                    log_keep = jnp.where(fresh, log_keep, 0.0)
                    tail = _dot(log_keep.astype(BF16), upper_blk)
                    a = jnp.where(fresh, jnp.exp2(log_beta + tail + carry), 0.0)
                    acc = acc + _dot(a.astype(BF16), v_ref[pl.ds(k0, tq), cols])
                    return k0, carry + row_sum(log_keep), acc

                _, _, out = lax.while_loop(more, step, (w0 + h * half, carry[rows], acc[rows]))
                o_ref[rows, cols] = out.astype(BF16)


def _attention(proj, *, bsz, seq, tq, n_heads):
    m = proj.shape[0]
    nq = seq // tq
    width = n_heads * HEAD_DIM
    groups = SEG // width
    return pl.pallas_call(
        functools.partial(_attn_kernel, tq=tq, n_heads=n_heads),
        grid=(bsz, groups, nq),
        in_specs=[
            pl.BlockSpec((tq, width), lambda b, h, i: (b * nq + i, h)),
            pl.BlockSpec((seq, width), lambda b, h, i: (b, groups + h)),
            pl.BlockSpec((seq, width), lambda b, h, i: (b, 2 * groups + h)),
        ],
        out_specs=pl.BlockSpec((tq, width), lambda b, h, i: (b * nq + i, h)),
        out_shape=jax.ShapeDtypeStruct((m, SEG), BF16),
        compiler_params=_params(("parallel", "parallel", "arbitrary")),
        name="sb_attention",
    )(proj, proj, proj)


def _merge_kernel(o_ref, u_ref, vln_ref, ws_ref, bs_ref, woa_ref, wob_ref, ga_ref, gb_ref,
                  out_ref, s_ref, *, tm):
    j = pl.program_id(1)

    def spatial_gating():
        ii = lax.broadcasted_iota(jnp.int32, (SGU_LEN, SGU_LEN), 0) // SGU_CHUNK
        jj = lax.broadcasted_iota(jnp.int32, (SGU_LEN, SGU_LEN), 1) // SGU_CHUNK
        allowed = jj <= ii
        for g in range(SEG // HEAD_DIM):
            sl = slice(g * HEAD_DIM, (g + 1) * HEAD_DIM)
            w = jnp.where(allowed, ws_ref[g], 0.0).astype(BF16)
            bias = bs_ref[:, g:g + 1]
            for c in range(tm // SGU_LEN):
                rows = slice(c * SGU_LEN, (c + 1) * SGU_LEN)
                mixed = _dot(w, vln_ref[rows, sl]) + bias
                s_ref[rows, sl] = (u_ref[rows, sl].astype(F32) * mixed).astype(BF16)

    def project():
        y_a = _dot(o_ref[...], woa_ref[...])
        y_b = _dot(s_ref[...], wob_ref[...])
        merged = ga_ref[...].astype(F32) * y_a + gb_ref[...].astype(F32) * y_b
        out_ref[...] = merged.astype(BF16)

    @pl.when(j == 0)
    def _():
        spatial_gating()
        project()

    @pl.when(j > 0)
    def _():
        project()


def _merge(o, proj, ws, bs_t, woa, wob, *, layer, tm, tn):
    m = o.shape[0]
    d = woa.shape[2]
    u_blk = 3
    v_blk = 4
    ga_blk = 5 * SEG // tn
    gb_blk = ga_blk + d // tn
    last = m // tm - 1

    def ahead(blk, from_step):
        return lambda i, j: (jnp.minimum(i + (j >= from_step).astype(jnp.int32), last), blk)

    return pl.pallas_call(
        functools.partial(_merge_kernel, tm=tm),
        grid=(m // tm, d // tn),
        in_specs=[
            pl.BlockSpec((tm, SEG), lambda i, j: (i, 0)),
            pl.BlockSpec((tm, SEG), ahead(u_blk, 1)),
            pl.BlockSpec((tm, SEG), ahead(v_blk, 2)),
            pl.BlockSpec((None,) + ws.shape[1:], lambda i, j: (layer, 0, 0, 0)),
            pl.BlockSpec((None,) + bs_t.shape[1:], lambda i, j: (layer, 0, 0)),
            pl.BlockSpec((None, SEG, tn), lambda i, j: (layer, 0, j)),
            pl.BlockSpec((None, SEG, tn), lambda i, j: (layer, 0, j)),
            pl.BlockSpec((tm, tn), lambda i, j: (i, ga_blk + j)),
            pl.BlockSpec((tm, tn), lambda i, j: (i, gb_blk + j)),
        ],
        out_specs=pl.BlockSpec((tm, tn), lambda i, j: (i, j)),
        out_shape=jax.ShapeDtypeStruct((m, d), BF16),
        scratch_shapes=[pltpu.VMEM((tm, SEG), BF16)],
        compiler_params=_params(("parallel", "arbitrary")),
        name="sgu_merge",
    )(o, proj, proj, ws, bs_t, woa, wob, proj, proj)


def _out_proj_kernel(x_ref, mg_ref, w_ref, g_ref, xo_ref, h_ref):
    x = x_ref[...] + _dot(mg_ref[...], w_ref[...])
    xo_ref[...] = x
    ms = jnp.mean(x * x, axis=-1, keepdims=True)
    h_ref[...] = (x * lax.rsqrt(ms + EPS) * g_ref[...]).astype(BF16)


def _out_proj(x, merged, w, g, *, layer, tm):
    m, d = x.shape
    return pl.pallas_call(
        _out_proj_kernel,
        grid=(m // tm,),
        in_specs=[
            pl.BlockSpec((tm, d), lambda i: (i, 0)),
            pl.BlockSpec((tm, d), lambda i: (i, 0)),
            pl.BlockSpec((None, d, d), lambda i: (layer, 0, 0)),
            pl.BlockSpec((None, 1, d), lambda i: (layer, 0, 0)),
        ],
        out_specs=[
            pl.BlockSpec((tm, d), lambda i: (i, 0)),
            pl.BlockSpec((tm, d), lambda i: (i, 0)),
        ],
        out_shape=[jax.ShapeDtypeStruct((m, d), F32), jax.ShapeDtypeStruct((m, d), BF16)],
        compiler_params=_params(("parallel",)),
        name="out_proj",
    )(x, merged, w, g)


def _ffn_kernel(x_ref, h_ref, w1_ref, w2_ref, o_ref, hid_ref, *, n_up):
    s = pl.program_id(1)

    @pl.when(s < n_up)
    def _():
        hid_ref[s] = jnp.square(jnp.maximum(_dot(h_ref[...], w1_ref[...]), 0.0)).astype(BF16)

    @pl.when(s >= n_up)
    def _():
        upd = _dot(hid_ref[0], w2_ref[0])
        for c in range(1, n_up):
            upd = upd + _dot(hid_ref[c], w2_ref[c])
        o_ref[...] = x_ref[...] + upd


def _ffn(x, h, w1, w2, *, layer, tm):
    m, d = x.shape
    _, n_down, n_up, tf, tn = w2.shape
    down = lambda s: jnp.maximum(s - n_up, 0)
    return pl.pallas_call(
        functools.partial(_ffn_kernel, n_up=n_up),
        grid=(m // tm, n_up + n_down),
        in_specs=[
            pl.BlockSpec((tm, tn), lambda i, s: (i, down(s))),
            pl.BlockSpec((tm, d), lambda i, s: (i, 0)),
            pl.BlockSpec((None, d, tf), lambda i, s: (layer, 0, jnp.minimum(s, n_up - 1))),
            pl.BlockSpec((None, None, n_up, tf, tn), lambda i, s: (layer, down(s), 0, 0, 0)),
        ],
        out_specs=pl.BlockSpec((tm, tn), lambda i, s: (i, down(s))),
        out_shape=jax.ShapeDtypeStruct((m, d), F32),
        scratch_shapes=[pltpu.VMEM((n_up, tm, tf), BF16)],
        compiler_params=_params(("parallel", "arbitrary")),
        name="ffn",
    )(x, h, w1, w2)


def _tile_w2_kernel(w_ref, o_ref):
    tn = o_ref.shape[-1]
    for t in range(o_ref.shape[0]):
        o_ref[t] = w_ref[:, t * tn:(t + 1) * tn].astype(BF16)


def _tile_w2(w2, *, tf, tn):
    depth, ff, d = w2.shape
    return pl.pallas_call(
        _tile_w2_kernel,
        grid=(depth, ff // tf),
        in_specs=[pl.BlockSpec((None, tf, d), lambda l, c: (l, c, 0))],
        out_specs=pl.BlockSpec((None, d // tn, None, tf, tn), lambda l, c: (l, 0, c, 0, 0)),
        out_shape=jax.ShapeDtypeStruct((depth, d // tn, ff // tf, tf, tn), BF16),
        compiler_params=_params(("parallel", "parallel")),
        name="tile_w2",
    )(w2)


def _tile(n, want):
    t = min(n, want)
    assert n % t == 0
    return t


def kernel(x, g_mix, w_in, g_q, g_k, sgu_ln_g, sgu_ln_b, w_spatial, b_spatial, w_oa, w_ob, w_out,
           g_ff, w_ff1, w_ff2):
    bsz, seq, d = x.shape
    depth = w_in.shape[0]
    m = bsz * seq
    assert seq % SGU_LEN == 0 and w_in.shape[2] == 5 * SEG + 2 * d
    xf = x.reshape(m, d)
    w_in, w_oa, w_ob, w_out, w_ff1 = (w.astype(BF16) for w in (w_in, w_oa, w_ob, w_out, w_ff1))
    w_ff2 = _tile_w2(w_ff2, tf=FFN_UP_TILE, tn=FFN_DOWN_TILE)
    g_mix, g_ff = g_mix.reshape(depth, 1, d), g_ff.reshape(depth, 1, d)
    g_q, g_k, sgu_ln_g, sgu_ln_b = (
        p.reshape(depth, 1, SEG) for p in (g_q, g_k, sgu_ln_g, sgu_ln_b))
    b_spatial_t = b_spatial.transpose(0, 2, 1)
    for l in range(depth):
        proj = _in_proj(xf, g_mix, w_in, g_q, g_k, sgu_ln_g, sgu_ln_b, layer=l, tm=_tile(m, 1024))
        o = _attention(proj, bsz=bsz, seq=seq, tq=_tile(seq, 256), n_heads=4)
        merged = _merge(o, proj, w_spatial, b_spatial_t, w_oa, w_ob, layer=l,
                        tm=_tile(m, 2048), tn=256)
        xf, h2 = _out_proj(xf, merged, w_out, g_ff, layer=l, tm=_tile(m, 512))
        xf = _ffn(xf, h2, w_ff1, w_ff2, layer=l, tm=_tile(m, 1024))
    return xf.reshape(bsz, seq, d)
```

```python
import functools

import jax
import jax.numpy as jnp
from jax import lax
from jax.experimental import pallas as pl
from jax.experimental.pallas import tpu as pltpu

EPS = 1e-6
HEAD_DIM = 128
SGU_LEN = 128
SGU_CHUNK = 64
SEG = 1024
MXU_WIDTH = 256
FFN_UP_TILE = 1024
FFN_DOWN_TILE = 256

VMEM_LIMIT_BYTES = 56 * 1024 * 1024

BF16 = jnp.bfloat16
F32 = jnp.float32

LOG2_E = 1.4426950408889634
STICK_ZERO_LOG2 = -160.0


def _params(sem):
    return pltpu.CompilerParams(dimension_semantics=sem, vmem_limit_bytes=VMEM_LIMIT_BYTES)


def _dot(a, b):
    return jnp.dot(a, b, preferred_element_type=F32)


def _gelu(x):
    return 0.5 * x * (1.0 + lax.erf(x * (2.0 ** -0.5)))


def _in_proj_kernel(x_ref, g_ref, w_ref, gq_ref, gk_ref, lng_ref, lnb_ref, o_ref, h_ref, *, scale):
    j = pl.program_id(1)
    chunks = [slice(c, c + MXU_WIDTH) for c in range(0, SEG, MXU_WIDTH)]

    @pl.when(j == 0)
    def _():
        x = x_ref[...]
        ms = jnp.mean(x * x, axis=-1, keepdims=True)
        h_ref[...] = (x * lax.rsqrt(ms + EPS) * g_ref[...]).astype(BF16)

    def chunk_dot(sl):
        return _dot(h_ref[...], w_ref[:, sl])

    def head_norm(gain_ref, mul):
        for sl in chunks:
            acc = chunk_dot(sl)
            for h in range(0, MXU_WIDTH, HEAD_DIM):
                blk = acc[:, h:h + HEAD_DIM]
                cols = slice(sl.start + h, sl.start + h + HEAD_DIM)
                ms = jnp.mean(blk * blk, axis=-1, keepdims=True)
                o_ref[:, cols] = (blk * (lax.rsqrt(ms + EPS) * mul) * gain_ref[:, cols]).astype(BF16)

    @pl.when(j == 0)
    def _():
        head_norm(gq_ref, scale)

    @pl.when(j == 1)
    def _():
        head_norm(gk_ref, 1.0)

    @pl.when(j == 2)
    def _():
        for sl in chunks:
            o_ref[:, sl] = chunk_dot(sl).astype(BF16)

    @pl.when(j == 3)
    def _():
        for sl in chunks:
            o_ref[:, sl] = _gelu(chunk_dot(sl)).astype(BF16)

    @pl.when(j == 4)
    def _():
        vs = [_gelu(chunk_dot(sl)) for sl in chunks]
        mu = sum(jnp.sum(v, axis=-1, keepdims=True) for v in vs) * (1.0 / SEG)
        vs = [v - mu for v in vs]
        var = sum(jnp.sum(v * v, axis=-1, keepdims=True) for v in vs) * (1.0 / SEG)
        inv = lax.rsqrt(var + EPS)
        for sl, v in zip(chunks, vs):
            o_ref[:, sl] = (v * inv * lng_ref[:, sl] + lnb_ref[:, sl]).astype(BF16)

    @pl.when(j >= 5)
    def _():
        for sl in chunks:
            o_ref[:, sl] = jax.nn.sigmoid(chunk_dot(sl)).astype(BF16)


def _in_proj(x, g, w, gq, gk, lng, lnb, *, layer, tm):
    m, d = x.shape
    n = w.shape[2]
    assert n % SEG == 0 and m % tm == 0
    vec = lambda width: pl.BlockSpec((None, 1, width), lambda i, j: (layer, 0, 0))
    return pl.pallas_call(
        functools.partial(_in_proj_kernel, scale=HEAD_DIM ** -0.5 * LOG2_E),
        grid=(m // tm, n // SEG),
        in_specs=[
            pl.BlockSpec((tm, d), lambda i, j: (i, 0)),
            vec(d),
            pl.BlockSpec((None, d, SEG), lambda i, j: (layer, 0, j)),
            vec(SEG), vec(SEG), vec(SEG), vec(SEG),
        ],
        out_specs=pl.BlockSpec((tm, SEG), lambda i, j: (i, j)),
        out_shape=jax.ShapeDtypeStruct((m, n), BF16),
        scratch_shapes=[pltpu.VMEM((tm, d), BF16)],
        compiler_params=_params(("parallel", "arbitrary")),
        name="in_proj",
    )(x, g, w, gq, gk, lng, lnb)


def _attn_kernel(q_ref, k_ref, v_ref, o_ref, *, tq, n_heads):
    half = tq // 2
    qi = pl.program_id(2)
    heads = [slice(h * HEAD_DIM, (h + 1) * HEAD_DIM) for h in range(n_heads)]

    def strictly_upper(n):
        return (lax.broadcasted_iota(jnp.int32, (n, n), 0)
                > lax.broadcasted_iota(jnp.int32, (n, n), 1)).astype(BF16)

    def log_sigmoids(z):
        soft = jnp.log(1.0 + jnp.exp2(-jnp.abs(z))) * LOG2_E
        log_beta = jnp.minimum(z, 0.0) - soft
        return log_beta, log_beta - z

    def qk(q, k):
        return lax.dot_general(q, k, (((1,), (1,)), ((), ())), preferred_element_type=F32)

    def row_sum(x):
        return jnp.sum(x, axis=-1, keepdims=True)

    @pl.when(qi == 0)
    def _():
        rows = lax.broadcasted_iota(jnp.int32, (tq, tq), 0)
        keys = lax.broadcasted_iota(jnp.int32, (tq, tq), 1)
        past = keys < rows
        upper = strictly_upper(tq)
        for cols in heads:
            log_beta, log_keep = log_sigmoids(qk(q_ref[:, cols], k_ref[pl.ds(0, tq), cols]))
            log_keep = jnp.where(past, log_keep, 0.0)
            a = jnp.where(past, jnp.exp2(log_beta + _dot(log_keep.astype(BF16), upper)), 0.0)
            o_ref[:, cols] = _dot(a.astype(BF16), v_ref[pl.ds(0, tq), cols]).astype(BF16)

    @pl.when(qi > 0)
    def _():
        win = 3 * half
        w0 = pl.multiple_of((qi - 1) * tq, tq)
        upper_win = strictly_upper(win)
        upper_blk = strictly_upper(tq)
        r = lax.broadcasted_iota(jnp.int32, (tq, half), 0)
        c = lax.broadcasted_iota(jnp.int32, (tq, half), 1)
        past = c < jnp.where(r >= half, r - half, r)
        key_in_blk = lax.broadcasted_iota(jnp.int32, (half, tq), 1)
        zero = jnp.zeros((half, half), BF16)

        def stacked_logits(cols):
            z = qk(q_ref[:, cols], k_ref[pl.ds(w0, 2 * tq), cols])
            return jnp.concatenate([z[:half, :win], z[half:, half:]], axis=0)

        def masked_keep(log_keep):
            return jnp.concatenate(
                [log_keep[:, :tq], jnp.where(past, log_keep[:, tq:], 0.0)], axis=1)

        def unstack(a):
            a = jnp.concatenate([a[:, :tq], jnp.where(past, a[:, tq:], 0.0)], axis=1).astype(BF16)
            return jnp.concatenate([jnp.concatenate([a[:half], zero], axis=1),
                                    jnp.concatenate([zero, a[half:]], axis=1)], axis=0)

        zs = [stacked_logits(cols) for cols in heads]
        logs = [log_sigmoids(z) for z in zs]
        keeps = [masked_keep(log_keep) for _, log_keep in logs]
        tails = [_dot(log_keep.astype(BF16), upper_win) for log_keep in keeps]
        weights = [unstack(jnp.exp2(log_beta + tail)) for (log_beta, _), tail in zip(logs, tails)]
        accs = [_dot(a, v_ref[pl.ds(w0, 2 * tq), cols]) for a, cols in zip(weights, heads)]
        states = [(row_sum(log_keep), acc) for log_keep, acc in zip(keeps, accs)]

        def more(state):
            end, carry, _ = state
            return jnp.logical_and(end > 0, jnp.max(carry) > STICK_ZERO_LOG2)

        halves = [slice(h * half, (h + 1) * half) for h in range(2)]
        unfinished = []
        for cols, (carry, acc) in zip(heads, states):
            o_ref[:, cols] = acc.astype(BF16)
            unfinished.append([more((w0 + h * half, carry[rows], None))
                               for h, rows in enumerate(halves)])

        for cols, (carry, acc), flags in zip(heads, states, unfinished):
            for h, rows in enumerate(halves):

                def step(state, cols=cols, rows=rows):
                    end, carry, acc = state
                    k0 = pl.multiple_of(jnp.maximum(end - tq, 0), half)
                    fresh = key_in_blk + k0 < end
                    log_beta, log_keep = log_sigmoids(
                        qk(q_ref[rows, cols], k_ref[pl.ds(k0, tq), cols]))
                    log_keep = jnp.where(fresh, log_keep, 0.0)
                    tail = _dot(log_keep.astype(BF16), upper_blk)
                    a = jnp.where(fresh, jnp.exp2(log_beta + tail + carry), 0.0)
                    acc = acc + _dot(a.astype(BF16), v_ref[pl.ds(k0, tq), cols])
                    return k0, carry + row_sum(log_keep), acc

                @pl.when(flags[h])
                def _(h=h, rows=rows, cols=cols, carry=carry, acc=acc, step=step):
                    _, _, out = lax.while_loop(more, step, (w0 + h * half, carry[rows], acc[rows]))
                    o_ref[rows, cols] = out.astype(BF16)


def _attention(proj, *, bsz, seq, tq, n_heads):
    m = proj.shape[0]
    nq = seq // tq
    width = n_heads * HEAD_DIM
    groups = SEG // width
    return pl.pallas_call(
        functools.partial(_attn_kernel, tq=tq, n_heads=n_heads),
        grid=(bsz, groups, nq),
        in_specs=[
            pl.BlockSpec((tq, width), lambda b, h, i: (b * nq + i, h)),
            pl.BlockSpec((seq, width), lambda b, h, i: (b, groups + h)),
            pl.BlockSpec((seq, width), lambda b, h, i: (b, 2 * groups + h)),
        ],
        out_specs=pl.BlockSpec((tq, width), lambda b, h, i: (b * nq + i, h)),
        out_shape=jax.ShapeDtypeStruct((m, SEG), BF16),
        compiler_params=_params(("parallel", "parallel", "arbitrary")),
        name="sb_attention",
    )(proj, proj, proj)


def _merge_kernel(o_ref, u_ref, vln_ref, ws_ref, bs_ref, woa_ref, wob_ref, ga_ref, gb_ref,
                  out_ref, s_ref, *, tm):
    j = pl.program_id(1)

    def spatial_gating():
        ii = lax.broadcasted_iota(jnp.int32, (SGU_LEN, SGU_LEN), 0) // SGU_CHUNK
        jj = lax.broadcasted_iota(jnp.int32, (SGU_LEN, SGU_LEN), 1) // SGU_CHUNK
        allowed = jj <= ii
        for g in range(SEG // HEAD_DIM):
            sl = slice(g * HEAD_DIM, (g + 1) * HEAD_DIM)
            w = jnp.where(allowed, ws_ref[g], 0.0).astype(BF16)
            bias = bs_ref[:, g:g + 1]
            for c in range(tm // SGU_LEN):
                rows = slice(c * SGU_LEN, (c + 1) * SGU_LEN)
                mixed = _dot(w, vln_ref[rows, sl]) + bias
                s_ref[rows, sl] = (u_ref[rows, sl].astype(F32) * mixed).astype(BF16)

    def project():
        y_a = _dot(o_ref[...], woa_ref[...])
        y_b = _dot(s_ref[...], wob_ref[...])
        merged = ga_ref[...].astype(F32) * y_a + gb_ref[...].astype(F32) * y_b
        out_ref[...] = merged.astype(BF16)

    @pl.when(j == 0)
    def _():
        spatial_gating()
        project()

    @pl.when(j > 0)
    def _():
        project()


def _merge(o, proj, ws, bs_t, woa, wob, *, layer, tm, tn):
    m = o.shape[0]
    d = woa.shape[2]
    u_blk = 3
    v_blk = 4
    ga_blk = 5 * SEG // tn
    gb_blk = ga_blk + d // tn
    last = m // tm - 1

    def ahead(blk, from_step):
        return lambda i, j: (jnp.minimum(i + (j >= from_step).astype(jnp.int32), last), blk)

    return pl.pallas_call(
        functools.partial(_merge_kernel, tm=tm),
        grid=(m // tm, d // tn),
        in_specs=[
            pl.BlockSpec((tm, SEG), lambda i, j: (i, 0)),
            pl.BlockSpec((tm, SEG), ahead(u_blk, 1)),
            pl.BlockSpec((tm, SEG), ahead(v_blk, 2)),
            pl.BlockSpec((None,) + ws.shape[1:], lambda i, j: (layer, 0, 0, 0)),
            pl.BlockSpec((None,) + bs_t.shape[1:], lambda i, j: (layer, 0, 0)),
            pl.BlockSpec((None, SEG, tn), lambda i, j: (layer, 0, j)),
            pl.BlockSpec((None, SEG, tn), lambda i, j: (layer, 0, j)),
            pl.BlockSpec((tm, tn), lambda i, j: (i, ga_blk + j)),
            pl.BlockSpec((tm, tn), lambda i, j: (i, gb_blk + j)),
        ],
        out_specs=pl.BlockSpec((tm, tn), lambda i, j: (i, j)),
        out_shape=jax.ShapeDtypeStruct((m, d), BF16),
        scratch_shapes=[pltpu.VMEM((tm, SEG), BF16)],
        compiler_params=_params(("parallel", "arbitrary")),
        name="sgu_merge",
    )(o, proj, proj, ws, bs_t, woa, wob, proj, proj)


def _out_proj_kernel(x_ref, mg_ref, w_ref, g_ref, xo_ref, h_ref):
    x = x_ref[...] + _dot(mg_ref[...], w_ref[...])
    xo_ref[...] = x
    ms = jnp.mean(x * x, axis=-1, keepdims=True)
    h_ref[...] = (x * lax.rsqrt(ms + EPS) * g_ref[...]).astype(BF16)


def _out_proj(x, merged, w, g, *, layer, tm):
    m, d = x.shape
    return pl.pallas_call(
        _out_proj_kernel,
        grid=(m // tm,),
        in_specs=[
            pl.BlockSpec((tm, d), lambda i: (i, 0)),
            pl.BlockSpec((tm, d), lambda i: (i, 0)),
            pl.BlockSpec((None, d, d), lambda i: (layer, 0, 0)),
            pl.BlockSpec((None, 1, d), lambda i: (layer, 0, 0)),
        ],
        out_specs=[
            pl.BlockSpec((tm, d), lambda i: (i, 0)),
            pl.BlockSpec((tm, d), lambda i: (i, 0)),
        ],
        out_shape=[jax.ShapeDtypeStruct((m, d), F32), jax.ShapeDtypeStruct((m, d), BF16)],
        compiler_params=_params(("parallel",)),
        name="out_proj",
    )(x, merged, w, g)


def _ffn_kernel(x_ref, h_ref, w1_ref, w2_ref, o_ref, hid_ref, *, n_up):
    s = pl.program_id(1)

    @pl.when(s < n_up)
    def _():
        hid_ref[s] = jnp.square(jnp.maximum(_dot(h_ref[...], w1_ref[...]), 0.0)).astype(BF16)

    @pl.when(s >= n_up)
    def _():
        upd = _dot(hid_ref[0], w2_ref[0])
        for c in range(1, n_up):
            upd = upd + _dot(hid_ref[c], w2_ref[c])
        o_ref[...] = x_ref[...] + upd


def _ffn(x, h, w1, w2, *, layer, tm):
    m, d = x.shape
    _, n_down, n_up, tf, tn = w2.shape
    down = lambda s: jnp.maximum(s - n_up, 0)
    return pl.pallas_call(
        functools.partial(_ffn_kernel, n_up=n_up),
        grid=(m // tm, n_up + n_down),
        in_specs=[
            pl.BlockSpec((tm, tn), lambda i, s: (i, down(s))),
            pl.BlockSpec((tm, d), lambda i, s: (i, 0)),
            pl.BlockSpec((None, d, tf), lambda i, s: (layer, 0, jnp.minimum(s, n_up - 1))),
            pl.BlockSpec((None, None, n_up, tf, tn), lambda i, s: (layer, down(s), 0, 0, 0)),
        ],
        out_specs=pl.BlockSpec((tm, tn), lambda i, s: (i, down(s))),
        out_shape=jax.ShapeDtypeStruct((m, d), F32),
        scratch_shapes=[pltpu.VMEM((n_up, tm, tf), BF16)],
        compiler_params=_params(("parallel", "arbitrary")),
        name="ffn",
    )(x, h, w1, w2)


def _tile_w2_kernel(w_ref, o_ref):
    tn = o_ref.shape[-1]
    for t in range(o_ref.shape[0]):
        o_ref[t] = w_ref[:, t * tn:(t + 1) * tn].astype(BF16)


def _tile_w2(w2, *, tf, tn):
    depth, ff, d = w2.shape
    return pl.pallas_call(
        _tile_w2_kernel,
        grid=(depth, ff // tf),
        in_specs=[pl.BlockSpec((None, tf, d), lambda l, c: (l, c, 0))],
        out_specs=pl.BlockSpec((None, d // tn, None, tf, tn), lambda l, c: (l, 0, c, 0, 0)),
        out_shape=jax.ShapeDtypeStruct((depth, d // tn, ff // tf, tf, tn), BF16),
        compiler_params=_params(("parallel", "parallel")),
        name="tile_w2",
    )(w2)


def _tile(n, want):
    t = min(n, want)
    assert n % t == 0
    return t


def kernel(x, g_mix, w_in, g_q, g_k, sgu_ln_g, sgu_ln_b, w_spatial, b_spatial, w_oa, w_ob, w_out,
           g_ff, w_ff1, w_ff2):
    bsz, seq, d = x.shape
    depth = w_in.shape[0]
    m = bsz * seq
    assert seq % SGU_LEN == 0 and w_in.shape[2] == 5 * SEG + 2 * d
    xf = x.reshape(m, d)
    w_in, w_oa, w_ob, w_out, w_ff1 = (w.astype(BF16) for w in (w_in, w_oa, w_ob, w_out, w_ff1))
    w_ff2 = _tile_w2(w_ff2, tf=FFN_UP_TILE, tn=FFN_DOWN_TILE)
    g_mix, g_ff = g_mix.reshape(depth, 1, d), g_ff.reshape(depth, 1, d)
    g_q, g_k, sgu_ln_g, sgu_ln_b = (
        p.reshape(depth, 1, SEG) for p in (g_q, g_k, sgu_ln_g, sgu_ln_b))
    b_spatial_t = b_spatial.transpose(0, 2, 1)
    for l in range(depth):
        proj = _in_proj(xf, g_mix, w_in, g_q, g_k, sgu_ln_g, sgu_ln_b, layer=l, tm=_tile(m, 1024))
        o = _attention(proj, bsz=bsz, seq=seq, tq=_tile(seq, 256), n_heads=4)
        merged = _merge(o, proj, w_spatial, b_spatial_t, w_oa, w_ob, layer=l,
                        tm=_tile(m, 2048), tn=256)
        xf, h2 = _out_proj(xf, merged, w_out, g_ff, layer=l, tm=_tile(m, 512))
        xf = _ffn(xf, h2, w_ff1, w_ff2, layer=l, tm=_tile(m, 1024))
    return xf.reshape(bsz, seq, d)
```

```python
import functools

import jax
import jax.numpy as jnp
from jax import lax
from jax.experimental import pallas as pl
from jax.experimental.pallas import tpu as pltpu

EPS = 1e-6
HEAD_DIM = 128
SGU_LEN = 128
SGU_CHUNK = 64
SEG = 1024
MXU_WIDTH = 256
FFN_UP_TILE = 1024
FFN_DOWN_TILE = 256

VMEM_LIMIT_BYTES = 56 * 1024 * 1024

BF16 = jnp.bfloat16
F32 = jnp.float32

LOG2_E = 1.4426950408889634
STICK_ZERO_LOG2 = -151.0


def _params(sem):
    return pltpu.CompilerParams(dimension_semantics=sem, vmem_limit_bytes=VMEM_LIMIT_BYTES)


def _dot(a, b):
    return jnp.dot(a, b, preferred_element_type=F32)


def _gelu(x):
    return 0.5 * x * (1.0 + lax.erf(x * (2.0 ** -0.5)))


def _in_proj_kernel(x_ref, g_ref, w_ref, gq_ref, gk_ref, lng_ref, lnb_ref, o_ref, h_ref, *, scale):
    j = pl.program_id(1)
    chunks = [slice(c, c + MXU_WIDTH) for c in range(0, SEG, MXU_WIDTH)]

    @pl.when(j == 0)
    def _():
        x = x_ref[...]
        ms = jnp.mean(x * x, axis=-1, keepdims=True)
        h_ref[...] = (x * lax.rsqrt(ms + EPS) * g_ref[...]).astype(BF16)

    def chunk_dot(sl):
        return _dot(h_ref[...], w_ref[:, sl])

    def head_norm(gain_ref, mul):
        for sl in chunks:
            acc = chunk_dot(sl)
            for h in range(0, MXU_WIDTH, HEAD_DIM):
                blk = acc[:, h:h + HEAD_DIM]
                cols = slice(sl.start + h, sl.start + h + HEAD_DIM)
                ms = jnp.mean(blk * blk, axis=-1, keepdims=True)
                o_ref[:, cols] = (blk * (lax.rsqrt(ms + EPS) * mul) * gain_ref[:, cols]).astype(BF16)

    @pl.when(j == 0)
    def _():
        head_norm(gq_ref, scale)

    @pl.when(j == 1)
    def _():
        head_norm(gk_ref, 1.0)

    @pl.when(j == 2)
    def _():
        for sl in chunks:
            o_ref[:, sl] = chunk_dot(sl).astype(BF16)

    @pl.when(j == 3)
    def _():
        for sl in chunks:
            o_ref[:, sl] = _gelu(chunk_dot(sl)).astype(BF16)

    @pl.when(j == 4)
    def _():
        vs = [_gelu(chunk_dot(sl)) for sl in chunks]
        mu = sum(jnp.sum(v, axis=-1, keepdims=True) for v in vs) * (1.0 / SEG)
        vs = [v - mu for v in vs]
        var = sum(jnp.sum(v * v, axis=-1, keepdims=True) for v in vs) * (1.0 / SEG)
        inv = lax.rsqrt(var + EPS)
        for sl, v in zip(chunks, vs):
            o_ref[:, sl] = (v * inv * lng_ref[:, sl] + lnb_ref[:, sl]).astype(BF16)

    @pl.when(j >= 5)
    def _():
        for sl in chunks:
            o_ref[:, sl] = jax.nn.sigmoid(chunk_dot(sl)).astype(BF16)


def _in_proj(x, g, w, gq, gk, lng, lnb, *, layer, tm):
    m, d = x.shape
    n = w.shape[2]
    assert n % SEG == 0 and m % tm == 0
    vec = lambda width: pl.BlockSpec((None, 1, width), lambda i, j: (layer, 0, 0))
    return pl.pallas_call(
        functools.partial(_in_proj_kernel, scale=HEAD_DIM ** -0.5 * LOG2_E),
        grid=(m // tm, n // SEG),
        in_specs=[
            pl.BlockSpec((tm, d), lambda i, j: (i, 0)),
            vec(d),
            pl.BlockSpec((None, d, SEG), lambda i, j: (layer, 0, j)),
            vec(SEG), vec(SEG), vec(SEG), vec(SEG),
        ],
        out_specs=pl.BlockSpec((tm, SEG), lambda i, j: (i, j)),
        out_shape=jax.ShapeDtypeStruct((m, n), BF16),
        scratch_shapes=[pltpu.VMEM((tm, d), BF16)],
        compiler_params=_params(("parallel", "arbitrary")),
        name="in_proj",
    )(x, g, w, gq, gk, lng, lnb)


def _attn_kernel(q_ref, k_ref, v_ref, o_ref, *, tq, n_heads):
    half = tq // 2
    qi = pl.program_id(2)
    heads = [slice(h * HEAD_DIM, (h + 1) * HEAD_DIM) for h in range(n_heads)]

    def strictly_upper(n):
        return (lax.broadcasted_iota(jnp.int32, (n, n), 0)
                > lax.broadcasted_iota(jnp.int32, (n, n), 1)).astype(BF16)

    def log_sigmoids(z):
        soft = jnp.log(1.0 + jnp.exp2(-jnp.abs(z))) * LOG2_E
        log_beta = jnp.minimum(z, 0.0) - soft
        return log_beta, log_beta - z

    def qk(q, k):
        return lax.dot_general(q, k, (((1,), (1,)), ((), ())), preferred_element_type=F32)

    def row_sum(x):
        return jnp.sum(x, axis=-1, keepdims=True)

    @pl.when(qi == 0)
    def _():
        rows = lax.broadcasted_iota(jnp.int32, (tq, tq), 0)
        keys = lax.broadcasted_iota(jnp.int32, (tq, tq), 1)
        past = keys < rows
        upper = strictly_upper(tq)
        for cols in heads:
            log_beta, log_keep = log_sigmoids(qk(q_ref[:, cols], k_ref[pl.ds(0, tq), cols]))
            log_keep = jnp.where(past, log_keep, 0.0)
            a = jnp.where(past, jnp.exp2(log_beta + _dot(log_keep.astype(BF16), upper)), 0.0)
            o_ref[:, cols] = _dot(a.astype(BF16), v_ref[pl.ds(0, tq), cols]).astype(BF16)

    @pl.when(qi > 0)
    def _():
        win = 3 * half
        w0 = pl.multiple_of((qi - 1) * tq, tq)
        upper_win = strictly_upper(win)
        upper_blk = strictly_upper(tq)
        r = lax.broadcasted_iota(jnp.int32, (tq, half), 0)
        c = lax.broadcasted_iota(jnp.int32, (tq, half), 1)
        past = c < jnp.where(r >= half, r - half, r)
        key_in_blk = lax.broadcasted_iota(jnp.int32, (half, tq), 1)
        zero = jnp.zeros((half, half), BF16)

        def stacked_logits(cols):
            z = qk(q_ref[:, cols], k_ref[pl.ds(w0, 2 * tq), cols])
            return jnp.concatenate([z[:half, :win], z[half:, half:]], axis=0)

        def masked_keep(log_keep):
            return jnp.concatenate(
                [log_keep[:, :tq], jnp.where(past, log_keep[:, tq:], 0.0)], axis=1)

        def unstack(a):
            a = jnp.concatenate([a[:, :tq], jnp.where(past, a[:, tq:], 0.0)], axis=1).astype(BF16)
            return jnp.concatenate([jnp.concatenate([a[:half], zero], axis=1),
                                    jnp.concatenate([zero, a[half:]], axis=1)], axis=0)

        zs = [stacked_logits(cols) for cols in heads]
        logs = [log_sigmoids(z) for z in zs]
        keeps = [masked_keep(log_keep) for _, log_keep in logs]
        tails = [_dot(log_keep.astype(BF16), upper_win) for log_keep in keeps]
        weights = [unstack(jnp.exp2(log_beta + tail)) for (log_beta, _), tail in zip(logs, tails)]
        accs = [_dot(a, v_ref[pl.ds(w0, 2 * tq), cols]) for a, cols in zip(weights, heads)]
        states = [(row_sum(log_keep), acc) for log_keep, acc in zip(keeps, accs)]

        def more(state):
            end, carry, _ = state
            return jnp.logical_and(end > 0, jnp.max(carry) > STICK_ZERO_LOG2)

        halves = [slice(h * half, (h + 1) * half) for h in range(2)]
        unfinished = []
        for cols, (carry, acc) in zip(heads, states):
            o_ref[:, cols] = acc.astype(BF16)
            unfinished.append([more((w0 + h * half, carry[rows], None))
                               for h, rows in enumerate(halves)])

        for cols, (carry, acc), flags in zip(heads, states, unfinished):
            for h, rows in enumerate(halves):

                def step(state, cols=cols, rows=rows):
                    end, carry, acc = state
                    k0 = pl.multiple_of(jnp.maximum(end - tq, 0), half)
                    fresh = key_in_blk + k0 < end
                    log_beta, log_keep = log_sigmoids(
                        qk(q_ref[rows, cols], k_ref[pl.ds(k0, tq), cols]))
                    log_keep = jnp.where(fresh, log_keep, 0.0)
                    tail = _dot(log_keep.astype(BF16), upper_blk)
                    a = jnp.where(fresh, jnp.exp2(log_beta + tail + carry), 0.0)
                    acc = acc + _dot(a.astype(BF16), v_ref[pl.ds(k0, tq), cols])
                    return k0, carry + row_sum(log_keep), acc

                @pl.when(flags[h])
                def _(h=h, rows=rows, cols=cols, carry=carry, acc=acc, step=step):
                    _, _, out = lax.while_loop(more, step, (w0 + h * half, carry[rows], acc[rows]))
                    o_ref[rows, cols] = out.astype(BF16)


def _attention(proj, *, bsz, seq, tq, n_heads):
    m = proj.shape[0]
    nq = seq // tq
    width = n_heads * HEAD_DIM
    groups = SEG // width
    return pl.pallas_call(
        functools.partial(_attn_kernel, tq=tq, n_heads=n_heads),
        grid=(bsz, groups, nq),
        in_specs=[
            pl.BlockSpec((tq, width), lambda b, h, i: (b * nq + i, h)),
            pl.BlockSpec((seq, width), lambda b, h, i: (b, groups + h)),
            pl.BlockSpec((seq, width), lambda b, h, i: (b, 2 * groups + h)),
        ],
        out_specs=pl.BlockSpec((tq, width), lambda b, h, i: (b * nq + i, h)),
        out_shape=jax.ShapeDtypeStruct((m, SEG), BF16),
        compiler_params=_params(("parallel", "parallel", "arbitrary")),
        name="sb_attention",
    )(proj, proj, proj)


def _merge_kernel(o_ref, u_ref, vln_ref, ws_ref, bs_ref, woa_ref, wob_ref, ga_ref, gb_ref,
                  out_ref, s_ref, *, tm):
    j = pl.program_id(1)

    def spatial_gating():
        ii = lax.broadcasted_iota(jnp.int32, (SGU_LEN, SGU_LEN), 0) // SGU_CHUNK
        jj = lax.broadcasted_iota(jnp.int32, (SGU_LEN, SGU_LEN), 1) // SGU_CHUNK
        allowed = jj <= ii
        for g in range(SEG // HEAD_DIM):
            sl = slice(g * HEAD_DIM, (g + 1) * HEAD_DIM)
            w = jnp.where(allowed, ws_ref[g], 0.0).astype(BF16)
            bias = bs_ref[:, g:g + 1]
            for c in range(tm // SGU_LEN):
                rows = slice(c * SGU_LEN, (c + 1) * SGU_LEN)
                mixed = _dot(w, vln_ref[rows, sl]) + bias
                s_ref[rows, sl] = (u_ref[rows, sl].astype(F32) * mixed).astype(BF16)

    def project():
        y_a = _dot(o_ref[...], woa_ref[...])
        y_b = _dot(s_ref[...], wob_ref[...])
        merged = ga_ref[...].astype(F32) * y_a + gb_ref[...].astype(F32) * y_b
        out_ref[...] = merged.astype(BF16)

    @pl.when(j == 0)
    def _():
        spatial_gating()
        project()

    @pl.when(j > 0)
    def _():
        project()


def _merge(o, proj, ws, bs_t, woa, wob, *, layer, tm, tn):
    m = o.shape[0]
    d = woa.shape[2]
    u_blk = 3
    v_blk = 4
    ga_blk = 5 * SEG // tn
    gb_blk = ga_blk + d // tn
    last = m // tm - 1

    def ahead(blk, from_step):
        return lambda i, j: (jnp.minimum(i + (j >= from_step).astype(jnp.int32), last), blk)

    return pl.pallas_call(
        functools.partial(_merge_kernel, tm=tm),
        grid=(m // tm, d // tn),
        in_specs=[
            pl.BlockSpec((tm, SEG), lambda i, j: (i, 0)),
            pl.BlockSpec((tm, SEG), ahead(u_blk, 1)),
            pl.BlockSpec((tm, SEG), ahead(v_blk, 2)),
            pl.BlockSpec((None,) + ws.shape[1:], lambda i, j: (layer, 0, 0, 0)),
            pl.BlockSpec((None,) + bs_t.shape[1:], lambda i, j: (layer, 0, 0)),
            pl.BlockSpec((None, SEG, tn), lambda i, j: (layer, 0, j)),
            pl.BlockSpec((None, SEG, tn), lambda i, j: (layer, 0, j)),
            pl.BlockSpec((tm, tn), lambda i, j: (i, ga_blk + j)),
            pl.BlockSpec((tm, tn), lambda i, j: (i, gb_blk + j)),
        ],
        out_specs=pl.BlockSpec((tm, tn), lambda i, j: (i, j)),
        out_shape=jax.ShapeDtypeStruct((m, d), BF16),
        scratch_shapes=[pltpu.VMEM((tm, SEG), BF16)],
        compiler_params=_params(("parallel", "arbitrary")),
        name="sgu_merge",
    )(o, proj, proj, ws, bs_t, woa, wob, proj, proj)


def _out_proj_kernel(x_ref, mg_ref, w_ref, g_ref, xo_ref, h_ref):
    x = x_ref[...] + _dot(mg_ref[...], w_ref[...])
    xo_ref[...] = x
    ms = jnp.mean(x * x, axis=-1, keepdims=True)
    h_ref[...] = (x * lax.rsqrt(ms + EPS) * g_ref[...]).astype(BF16)


def _out_proj(x, merged, w, g, *, layer, tm):
    m, d = x.shape
    return pl.pallas_call(
        _out_proj_kernel,
        grid=(m // tm,),
        in_specs=[
            pl.BlockSpec((tm, d), lambda i: (i, 0)),
            pl.BlockSpec((tm, d), lambda i: (i, 0)),
            pl.BlockSpec((None, d, d), lambda i: (layer, 0, 0)),
            pl.BlockSpec((None, 1, d), lambda i: (layer, 0, 0)),
        ],
        out_specs=[
            pl.BlockSpec((tm, d), lambda i: (i, 0)),
            pl.BlockSpec((tm, d), lambda i: (i, 0)),
        ],
        out_shape=[jax.ShapeDtypeStruct((m, d), F32), jax.ShapeDtypeStruct((m, d), BF16)],
        compiler_params=_params(("parallel",)),
        name="out_proj",
    )(x, merged, w, g)


def _ffn_kernel(x_ref, h_ref, w1_ref, w2_ref, o_ref, hid_ref, *, n_up):
    s = pl.program_id(1)

    @pl.when(s < n_up)
    def _():
        hid_ref[s] = jnp.square(jnp.maximum(_dot(h_ref[...], w1_ref[...]), 0.0)).astype(BF16)

    @pl.when(s >= n_up)
    def _():
        upd = _dot(hid_ref[0], w2_ref[0])
        for c in range(1, n_up):
            upd = upd + _dot(hid_ref[c], w2_ref[c])
        o_ref[...] = x_ref[...] + upd


def _ffn(x, h, w1, w2, *, layer, tm):
    m, d = x.shape
    _, n_down, n_up, tf, tn = w2.shape
    down = lambda s: jnp.maximum(s - n_up, 0)
    return pl.pallas_call(
        functools.partial(_ffn_kernel, n_up=n_up),
        grid=(m // tm, n_up + n_down),
        in_specs=[
            pl.BlockSpec((tm, tn), lambda i, s: (i, down(s))),
            pl.BlockSpec((tm, d), lambda i, s: (i, 0)),
            pl.BlockSpec((None, d, tf), lambda i, s: (layer, 0, jnp.minimum(s, n_up - 1))),
            pl.BlockSpec((None, None, n_up, tf, tn), lambda i, s: (layer, down(s), 0, 0, 0)),
        ],
        out_specs=pl.BlockSpec((tm, tn), lambda i, s: (i, down(s))),
        out_shape=jax.ShapeDtypeStruct((m, d), F32),
        scratch_shapes=[pltpu.VMEM((n_up, tm, tf), BF16)],
        compiler_params=_params(("parallel", "arbitrary")),
        name="ffn",
    )(x, h, w1, w2)


def _tile_w2_kernel(w_ref, o_ref):
    tn = o_ref.shape[-1]
    for t in range(o_ref.shape[0]):
        o_ref[t] = w_ref[:, t * tn:(t + 1) * tn].astype(BF16)


def _tile_w2(w2, *, tf, tn):
    depth, ff, d = w2.shape
    return pl.pallas_call(
        _tile_w2_kernel,
        grid=(depth, ff // tf),
        in_specs=[pl.BlockSpec((None, tf, d), lambda l, c: (l, c, 0))],
        out_specs=pl.BlockSpec((None, d // tn, None, tf, tn), lambda l, c: (l, 0, c, 0, 0)),
        out_shape=jax.ShapeDtypeStruct((depth, d // tn, ff // tf, tf, tn), BF16),
        compiler_params=_params(("parallel", "parallel")),
        name="tile_w2",
    )(w2)


def _tile(n, want):
    t = min(n, want)
    assert n % t == 0
    return t


def kernel(x, g_mix, w_in, g_q, g_k, sgu_ln_g, sgu_ln_b, w_spatial, b_spatial, w_oa, w_ob, w_out,
           g_ff, w_ff1, w_ff2):
    bsz, seq, d = x.shape
    depth = w_in.shape[0]
    m = bsz * seq
    assert seq % SGU_LEN == 0 and w_in.shape[2] == 5 * SEG + 2 * d
    xf = x.reshape(m, d)
    w_in, w_oa, w_ob, w_out, w_ff1 = (w.astype(BF16) for w in (w_in, w_oa, w_ob, w_out, w_ff1))
    w_ff2 = _tile_w2(w_ff2, tf=FFN_UP_TILE, tn=FFN_DOWN_TILE)
    g_mix, g_ff = g_mix.reshape(depth, 1, d), g_ff.reshape(depth, 1, d)
    g_q, g_k, sgu_ln_g, sgu_ln_b = (
        p.reshape(depth, 1, SEG) for p in (g_q, g_k, sgu_ln_g, sgu_ln_b))
    b_spatial_t = b_spatial.transpose(0, 2, 1)
    for l in range(depth):
        proj = _in_proj(xf, g_mix, w_in, g_q, g_k, sgu_ln_g, sgu_ln_b, layer=l, tm=_tile(m, 1024))
        o = _attention(proj, bsz=bsz, seq=seq, tq=_tile(seq, 256), n_heads=4)
        merged = _merge(o, proj, w_spatial, b_spatial_t, w_oa, w_ob, layer=l,
                        tm=_tile(m, 2048), tn=256)
        xf, h2 = _out_proj(xf, merged, w_out, g_ff, layer=l, tm=_tile(m, 512))
        xf = _ffn(xf, h2, w_ff1, w_ff2, layer=l, tm=_tile(m, 1024))
    return xf.reshape(bsz, seq, d)
```
